```python
import math
import jax, jax.numpy as jnp
from jax import lax
import numpy as np

D_MODEL = 4096
BATCH = 4
SEQ = 2048
DEPTH = 4

N_META = 16
BLOCK_Q = 128
HEAD_DIM = 128
CONV_WIDTH = D_MODEL // 4
DIFF_HEADS = (3 * D_MODEL) // (8 * HEAD_DIM)
SB_HEADS = (3 * D_MODEL) // (8 * HEAD_DIM)
DIFF_WIDTH = DIFF_HEADS * HEAD_DIM
SB_WIDTH = SB_HEADS * HEAD_DIM
MIX_WIDTH = CONV_WIDTH + DIFF_WIDTH + SB_WIDTH
DIFF_QK_DIM = HEAD_DIM // 2
IN_WIDTH = 3 * CONV_WIDTH + 3 * DIFF_WIDTH + 3 * SB_WIDTH
SHORT_CONV_K = 3
FFN_CONV_K = 3
D_FF = ((8 * D_MODEL // 3 + 255) // 256) * 256
DN_ALPHA = (2 * DEPTH) ** 0.25
DN_BETA = (8 * DEPTH) ** -0.25
LN_EPS = 1e-5

kernel_name = "hymba_style_conv_diff_stickbreak_deepnorm"


def lambda_init(layer):
    return 0.8 - 0.6 * math.exp(-0.3 * layer)


def layer_norm(x, g, b):
    xf = x.astype(jnp.float32)
    mu = jnp.mean(xf, axis=-1, keepdims=True)
    var = jnp.mean(jnp.square(xf - mu), axis=-1, keepdims=True)
    y = (xf - mu) * lax.rsqrt(var + LN_EPS)
    return (y * g.astype(jnp.float32) + b.astype(jnp.float32)).astype(x.dtype)


def causal_dwconv(x, w):
    k_size = w.shape[0]
    length = x.shape[1]
    xp = jnp.pad(x, ((0, 0), (k_size - 1, 0), (0, 0)))
    return sum(w[k] * xp[:, k:k + length] for k in range(k_size))


def query_blocks(length):
    bounds = [(0, N_META)]
    start = N_META
    while start < length:
        end = min(start + BLOCK_Q, length)
        bounds.append((start, end))
        start = end
    return bounds


def diff_attention(q, k, v, lam_q1, lam_k1, lam_q2, lam_k2, norm_g, lam_init):
    bsz, length, n_heads = v.shape[:3]
    q = q.transpose(0, 2, 3, 1, 4)
    k = k.transpose(0, 2, 3, 1, 4)
    v = v.transpose(0, 2, 1, 3)
    lam = (jnp.exp(jnp.sum(lam_q1.astype(jnp.float32) * lam_k1.astype(jnp.float32)))
           - jnp.exp(jnp.sum(lam_q2.astype(jnp.float32) * lam_k2.astype(jnp.float32)))
           + lam_init)
    scale = DIFF_QK_DIM ** -0.5
    outs = []
    for s, e in query_blocks(length):
        scores = jnp.einsum('bhcqd,bhckd->bhcqk', q[:, :, :, s:e], k[:, :, :, :e]).astype(jnp.float32) * scale
        mask = jnp.arange(e)[None, :] <= jnp.arange(s, e)[:, None]
        probs = jax.nn.softmax(jnp.where(mask, scores, -jnp.inf), axis=-1)
        w = probs[:, :, 0] - lam * probs[:, :, 1]
        outs.append(jnp.einsum('bhqk,bhkd->bhqd', w.astype(v.dtype), v[:, :, :e]))
    o = jnp.concatenate(outs, axis=2).astype(jnp.float32)
    o = o * lax.rsqrt(jnp.mean(jnp.square(o), axis=-1, keepdims=True) + LN_EPS)
    o = (o * norm_g.astype(jnp.float32) * (1.0 - lam_init)).astype(v.dtype)
    return o.transpose(0, 2, 1, 3).reshape(bsz, length, n_heads * v.shape[-1])


def stick_breaking_attention(q, k, v):
    bsz, length, n_heads, hd = v.shape
    q = q.transpose(0, 2, 1, 3)
    k = k.transpose(0, 2, 1, 3)
    v = v.transpose(0, 2, 1, 3)
    scale = hd ** -0.5
    outs = []
    for s, e in query_blocks(length):
        z = jnp.einsum('bhqd,bhkd->bhqk', q[:, :, s:e], k[:, :, :e]).astype(jnp.float32) * scale
        mask = jnp.arange(e)[None, :] < jnp.arange(s, e)[:, None]
        log_keep = jnp.where(mask, jax.nn.log_sigmoid(-z), 0.0)
        tail = lax.cumsum(log_keep, axis=3, reverse=True) - log_keep
        weights = jnp.where(mask, jnp.exp(jax.nn.log_sigmoid(z) + tail), 0.0)
        outs.append(jnp.einsum('bhqk,bhkd->bhqd', weights.astype(v.dtype), v[:, :, :e]))
    o = jnp.concatenate(outs, axis=2)
    return o.transpose(0, 2, 1, 3).reshape(bsz, length, n_heads * hd)


def conv_glu_ffn(x, w_up, conv_w, w_down):
    u = causal_dwconv(x @ w_up, conv_w)
    gate, up = jnp.split(u, 2, axis=-1)
    return (jax.nn.silu(gate) * up) @ w_down


def setup_inputs(seed: int = 0) -> dict:
    key = jax.random.key(seed)
    ks = jax.random.split(key, 20)
    f32 = jnp.float32
    nrm = lambda k, shape, s: jax.random.normal(k, shape, f32) * s
    return {
        "x": nrm(ks[0], (BATCH, SEQ, D_MODEL), 1.0),
        "meta_tokens": nrm(ks[1], (N_META, D_MODEL), 1.0),
        "emb_ln_g": 1.0 + nrm(ks[2], (D_MODEL,), 0.02),
        "emb_ln_b": nrm(ks[3], (D_MODEL,), 0.02),
        "w_in": nrm(ks[4], (DEPTH, D_MODEL, IN_WIDTH), D_MODEL ** -0.5),
        "short_conv_w": nrm(ks[5], (DEPTH, SHORT_CONV_K, CONV_WIDTH), SHORT_CONV_K ** -0.5),
        "lambda_q1": nrm(ks[6], (DEPTH, DIFF_QK_DIM), 0.1),
        "lambda_k1": nrm(ks[7], (DEPTH, DIFF_QK_DIM), 0.1),
        "lambda_q2": nrm(ks[8], (DEPTH, DIFF_QK_DIM), 0.1),
        "lambda_k2": nrm(ks[9], (DEPTH, DIFF_QK_DIM), 0.1),
        "diff_norm_g": 1.0 + nrm(ks[10], (DEPTH, HEAD_DIM), 0.02),
        "w_out": nrm(ks[11], (DEPTH, MIX_WIDTH, D_MODEL), DN_BETA * MIX_WIDTH ** -0.5),
        "ln1_g": 1.0 + nrm(ks[12], (DEPTH, D_MODEL), 0.02),
        "ln1_b": nrm(ks[13], (DEPTH, D_MODEL), 0.02),
        "w_up": nrm(ks[14], (DEPTH, D_MODEL, 2 * D_FF), D_MODEL ** -0.5),
        "ffn_conv_w": nrm(ks[15], (DEPTH, FFN_CONV_K, 2 * D_FF), FFN_CONV_K ** -0.5),
        "w_down": nrm(ks[16], (DEPTH, D_FF, D_MODEL), DN_BETA * D_FF ** -0.5),
        "ln2_g": 1.0 + nrm(ks[17], (DEPTH, D_MODEL), 0.02),
        "ln2_b": nrm(ks[18], (DEPTH, D_MODEL), 0.02),
    }


def reference(x, meta_tokens, emb_ln_g, emb_ln_b, w_in, short_conv_w, lambda_q1, lambda_k1,
              lambda_q2, lambda_k2, diff_norm_g, w_out, ln1_g, ln1_b, w_up, ffn_conv_w, w_down,
              ln2_g, ln2_b):
    bsz = x.shape[0]
    meta = jnp.broadcast_to(meta_tokens[None].astype(x.dtype), (bsz, N_META, x.shape[-1]))
    h = layer_norm(jnp.concatenate([meta, x], axis=1), emb_ln_g, emb_ln_b)
    length = h.shape[1]
    sizes = [CONV_WIDTH] * 3 + [DIFF_WIDTH] * 3 + [SB_WIDTH] * 3
    split_idx = [int(i) for i in np.cumsum(sizes)[:-1]]
    for l in range(DEPTH):
        proj = h @ w_in[l]
        cb, cc, ch, dq, dk, dv, sq, sk, sv = jnp.split(proj, split_idx, axis=-1)
        y_conv = cb * causal_dwconv(cc * ch, short_conv_w[l])
        y_diff = diff_attention(
            dq.reshape(bsz, length, DIFF_HEADS, 2, DIFF_QK_DIM),
            dk.reshape(bsz, length, DIFF_HEADS, 2, DIFF_QK_DIM),
            dv.reshape(bsz, length, DIFF_HEADS, HEAD_DIM),
            lambda_q1[l], lambda_k1[l], lambda_q2[l], lambda_k2[l], diff_norm_g[l], lambda_init(l))
        y_sb = stick_breaking_attention(
            sq.reshape(bsz, length, SB_HEADS, HEAD_DIM),
            sk.reshape(bsz, length, SB_HEADS, HEAD_DIM),
            sv.reshape(bsz, length, SB_HEADS, HEAD_DIM))
        mix = jnp.concatenate([y_conv, y_diff, y_sb], axis=-1) @ w_out[l]
        h = layer_norm(DN_ALPHA * h + mix, ln1_g[l], ln1_b[l])
        h = layer_norm(DN_ALPHA * h + conv_glu_ffn(h, w_up[l], ffn_conv_w[l], w_down[l]), ln2_g[l], ln2_b[l])
    return h[:, N_META:]
```

```python
import functools
import math

import jax
import jax.numpy as jnp
from jax import lax
from jax.experimental import pallas as pl
from jax.experimental.pallas import tpu as pltpu

N_META = 16
HEAD_DIM = 128
DIFF_QK_DIM = HEAD_DIM // 2
LN_EPS = 1e-5
ATT_BLOCK = 128
CONV_HALO = 8
VMEM_LIMIT = 56 * 1024 * 1024

F32 = jnp.float32
BF16 = jnp.bfloat16


def _lambda_init(layer):
    return 0.8 - 0.6 * math.exp(-0.3 * layer)


def _pick_tile(n, candidates):
    return next(c for c in candidates if n % c == 0)


def _params(*sem):
    return pltpu.CompilerParams(dimension_semantics=sem, vmem_limit_bytes=VMEM_LIMIT)


def _dot(a, b):
    return jnp.dot(a, b, preferred_element_type=F32)


def _dot_nt(a, b):
    return lax.dot_general(a, b, (((1,), (1,)), ((), ())), preferred_element_type=F32)


def _ln_kernel(x_ref, g_ref, b_ref, of_ref, ob_ref):
    x = x_ref[...]
    mu = jnp.mean(x, axis=-1, keepdims=True)
    xc = x - mu
    var = jnp.mean(xc * xc, axis=-1, keepdims=True)
    y = xc * lax.rsqrt(var + LN_EPS) * g_ref[...] + b_ref[...]
    of_ref[...] = y
    ob_ref[...] = y.astype(BF16)


def _layer_norm(x, g, b, *, tr):
    rows, d = x.shape
    return pl.pallas_call(
        _ln_kernel,
        grid=(rows // tr,),
        in_specs=[pl.BlockSpec((tr, d), lambda i: (i, 0)),
                  pl.BlockSpec((1, d), lambda i: (0, 0)),
                  pl.BlockSpec((1, d), lambda i: (0, 0))],
        out_specs=[pl.BlockSpec((tr, d), lambda i: (i, 0)),
                   pl.BlockSpec((tr, d), lambda i: (i, 0))],
        out_shape=[jax.ShapeDtypeStruct((rows, d), F32), jax.ShapeDtypeStruct((rows, d), BF16)],
        compiler_params=_params("parallel"),
        name="layer_norm",
    )(x, g.reshape(1, d), b.reshape(1, d))


def _mm_kernel(x_ref, w_ref, o_ref):
    o_ref[...] = _dot(x_ref[...], w_ref[...]).astype(o_ref.dtype)


def _project(x, w, layer, *, col_off, n_cols, tm, tn, out_dtype):
    m, k = x.shape
    off = col_off // tn
    return pl.pallas_call(
        _mm_kernel,
        grid=(m // tm, n_cols // tn),
        in_specs=[pl.BlockSpec((tm, k), lambda i, j: (i, 0)),
                  pl.BlockSpec((None, k, tn), lambda i, j: (layer, 0, j + off))],
        out_specs=pl.BlockSpec((tm, tn), lambda i, j: (i, j)),
        out_shape=jax.ShapeDtypeStruct((m, n_cols), out_dtype),
        compiler_params=_params("parallel", "arbitrary"),
        name="proj_attn",
    )(x, w)


def _causal_conv3(u, w_ref, scr_ref, halo_ref, n, first_tile):
    tm = u.shape[0]

    @pl.when(first_tile)
    def _():
        halo_ref[n] = jnp.zeros(halo_ref.shape[1:], F32)

    scr_ref[0:CONV_HALO, :] = halo_ref[n]
    scr_ref[CONV_HALO:CONV_HALO + tm, :] = u
    halo_ref[n] = u[tm - CONV_HALO:tm, :]
    w = w_ref[...]
    return (w[0:1, :] * scr_ref[CONV_HALO - 2:CONV_HALO - 2 + tm, :]
            + w[1:2, :] * scr_ref[CONV_HALO - 1:CONV_HALO - 1 + tm, :]
            + w[2:3, :] * u)


def _conv_mixer_kernel(x_ref, wb_ref, wc_ref, wh_ref, cw_ref, o_ref, scr_ref, halo_ref, *, tiles_per_seq):
    i, n = pl.program_id(0), pl.program_id(1)
    x = x_ref[...]
    cb = _dot(x, wb_ref[...])
    g = _dot(x, wc_ref[...]) * _dot(x, wh_ref[...])
    y = _causal_conv3(g, cw_ref, scr_ref, halo_ref, n, i % tiles_per_seq == 0)
    o_ref[...] = (cb * y).astype(o_ref.dtype)


def _conv_mixer(x, w_in, conv_w, layer, *, width, seq, tm, tn):
    m, k = x.shape
    nt = width // tn
    kern = functools.partial(_conv_mixer_kernel, tiles_per_seq=seq // tm)
    wspec = lambda s: pl.BlockSpec((None, k, tn), lambda i, j: (layer, 0, j + s * nt))
    return pl.pallas_call(
        kern,
        grid=(m // tm, nt),
        in_specs=[pl.BlockSpec((tm, k), lambda i, j: (i, 0)), wspec(0), wspec(1), wspec(2),
                  pl.BlockSpec((None, 3, tn), lambda i, j: (layer, 0, j))],
        out_specs=pl.BlockSpec((tm, tn), lambda i, j: (i, j)),
        out_shape=jax.ShapeDtypeStruct((m, width), BF16),
        scratch_shapes=[pltpu.VMEM((CONV_HALO + tm, tn), F32), pltpu.VMEM((nt, CONV_HALO, tn), F32)],
        compiler_params=_params("arbitrary", "arbitrary"),
        name="conv_mixer",
    )(x, w_in, w_in, w_in, conv_w)


def _ffn_up_kernel(x_ref, wg_ref, wu_ref, cg_ref, cu_ref, o_ref, scr_g, scr_u, halo_g, halo_u, *, tiles_per_seq):
    i, n = pl.program_id(0), pl.program_id(1)
    first = i % tiles_per_seq == 0
    x = x_ref[...]
    gate = _causal_conv3(_dot(x, wg_ref[...]), cg_ref, scr_g, halo_g, n, first)
    up = _causal_conv3(_dot(x, wu_ref[...]), cu_ref, scr_u, halo_u, n, first)
    o_ref[...] = (gate * jax.nn.sigmoid(gate) * up).astype(o_ref.dtype)


def _ffn_up(x, w_up, conv_w, layer, *, d_ff, seq, tm, tn):
    m, k = x.shape
    nt = d_ff // tn
    kern = functools.partial(_ffn_up_kernel, tiles_per_seq=seq // tm)
    return pl.pallas_call(
        kern,
        grid=(m // tm, nt),
        in_specs=[pl.BlockSpec((tm, k), lambda i, j: (i, 0)),
                  pl.BlockSpec((None, k, tn), lambda i, j: (layer, 0, j)),
                  pl.BlockSpec((None, k, tn), lambda i, j: (layer, 0, j + nt)),
                  pl.BlockSpec((None, 3, tn), lambda i, j: (layer, 0, j)),
                  pl.BlockSpec((None, 3, tn), lambda i, j: (layer, 0, j + nt))],
        out_specs=pl.BlockSpec((tm, tn), lambda i, j: (i, j)),
        out_shape=jax.ShapeDtypeStruct((m, d_ff), BF16),
        scratch_shapes=[pltpu.VMEM((CONV_HALO + tm, tn), F32), pltpu.VMEM((CONV_HALO + tm, tn), F32),
                        pltpu.VMEM((nt, CONV_HALO, tn), F32), pltpu.VMEM((nt, CONV_HALO, tn), F32)],
        compiler_params=_params("arbitrary", "arbitrary"),
        name="ffn_up",
    )(x, w_up, w_up, conv_w, conv_w)


def _out_proj_kernel(yc_ref, yd_ref, ys_ref, wc_ref, wd_ref, ws_ref, h_ref, o_ref, *, alpha):
    mix = _dot(yc_ref[...], wc_ref[...]) + _dot(yd_ref[...], wd_ref[...]) + _dot(ys_ref[...], ws_ref[...])
    o_ref[...] = alpha * h_ref[...] + mix


def _out_proj(yc, yd, ys, wc, wd, ws, h, layer, *, alpha, tm, tn):
    m, d = h.shape
    lhs = lambda a: pl.BlockSpec((tm, a.shape[1]), lambda i, j: (i, 0))
    rhs = lambda w: pl.BlockSpec((None, w.shape[1], tn), lambda i, j: (layer, 0, j))
    return pl.pallas_call(
        functools.partial(_out_proj_kernel, alpha=alpha),
        grid=(m // tm, d // tn),
        in_specs=[lhs(yc), lhs(yd), lhs(ys), rhs(wc), rhs(wd), rhs(ws),
                  pl.BlockSpec((tm, tn), lambda i, j: (i, j))],
        out_specs=pl.BlockSpec((tm, tn), lambda i, j: (i, j)),
        out_shape=jax.ShapeDtypeStruct((m, d), F32),
        compiler_params=_params("parallel", "arbitrary"),
        name="out_proj",
    )(yc, yd, ys, wc, wd, ws, h)


def _mm_res_kernel(x_ref, w_ref, h_ref, o_ref, *, alpha):
    o_ref[...] = alpha * h_ref[...] + _dot(x_ref[...], w_ref[...])


def _ffn_down(a, w, h, layer, *, alpha, tm, tn):
    m, d = h.shape
    k = a.shape[1]
    return pl.pallas_call(
        functools.partial(_mm_res_kernel, alpha=alpha),
        grid=(m // tm, d // tn),
        in_specs=[pl.BlockSpec((tm, k), lambda i, j: (i, 0)),
                  pl.BlockSpec((None, k, tn), lambda i, j: (layer, 0, j)),
                  pl.BlockSpec((tm, tn), lambda i, j: (i, j))],
        out_specs=pl.BlockSpec((tm, tn), lambda i, j: (i, j)),
        out_shape=jax.ShapeDtypeStruct((m, d), F32),
        compiler_params=_params("parallel", "arbitrary"),
        name="ffn_down",
    )(a, w, h)


def _pad_rows(a, rows):
    return jnp.concatenate([a, jnp.zeros((rows - a.shape[0], a.shape[1]), a.dtype)], axis=0)


def _block_iota(rows, cols, t):
    r = lax.broadcasted_iota(jnp.int32, (rows, cols), 0)
    c = lax.broadcasted_iota(jnp.int32, (rows, cols), 1)
    return jnp.where(r >= t, r - t, r), c


def _diff_attn_kernel(lam_ref, q_ref, k_ref, v_ref, g_ref, o_ref, *, seq, gain):
    blk = ATT_BLOCK
    n_full, tail = seq // blk, seq % blk
    lam = lam_ref[0]
    g = g_ref[...]
    lo_lane = lax.broadcasted_iota(jnp.int32, (1, HEAD_DIM), 1) < DIFF_QK_DIM

    def stacked_q(q0, t):
        q = (q_ref[pl.ds(q0, t), :].astype(F32) * (DIFF_QK_DIM ** -0.5)).astype(BF16)
        zero = jnp.zeros_like(q)
        return jnp.concatenate([jnp.where(lo_lane, q, zero), jnp.where(lo_lane, zero, q)], axis=0)

    def update(s, v, m, l, acc):
        m_new = jnp.maximum(m, jnp.max(s, axis=-1, keepdims=True))
        a = jnp.exp(m - m_new)
        p = jnp.exp(s - m_new)
        l = a * l + jnp.sum(p, axis=-1, keepdims=True)
        acc = a * acc + _dot(p.astype(BF16), v)
        return m_new, l, acc

    def q_block(q0, t, n_prev):
        qs = stacked_q(q0, t)
        init = (jnp.full((2 * t, 1), -jnp.inf, F32), jnp.zeros((2 * t, 1), F32), jnp.zeros((2 * t, HEAD_DIM), F32))

        def kv_step(j, carry):
            k0 = pl.multiple_of(j * blk, blk)
            return update(_dot_nt(qs, k_ref[pl.ds(k0, blk), :]), v_ref[pl.ds(k0, blk), :], *carry)

        m, l, acc = lax.fori_loop(0, n_prev, kv_step, init)
        kd, vd = k_ref[pl.ds(q0, t), :], v_ref[pl.ds(q0, t), :]
        if t != blk:
            kd, vd = _pad_rows(kd, blk), _pad_rows(vd, blk)
        s = _dot_nt(qs, kd)
        r, c = _block_iota(2 * t, blk, t)
        m, l, acc = update(jnp.where(c <= r, s, -jnp.inf), vd, m, l, acc)
        o = acc[:t] / l[:t] - lam * (acc[t:] / l[t:])
        o = o * lax.rsqrt(jnp.mean(o * o, axis=-1, keepdims=True) + LN_EPS)
        o_ref[pl.ds(q0, t), :] = (o * g * gain).astype(o_ref.dtype)

    def full_block(i, _):
        q_block(pl.multiple_of(i * blk, blk), blk, i)
        return 0

    lax.fori_loop(0, n_full, full_block, 0)
    if tail:
        q_block(n_full * blk, tail, n_full)


def _diff_attention(proj, lam, gain_g, *, heads, col_off, lam_init):
    b, seq, _ = proj.shape
    c0 = col_off // HEAD_DIM
    spec = lambda s: pl.BlockSpec((None, seq, HEAD_DIM), lambda i, h: (i, 0, c0 + s * heads + h))
    return pl.pallas_call(
        functools.partial(_diff_attn_kernel, seq=seq, gain=1.0 - lam_init),
        grid=(b, heads),
        in_specs=[pl.BlockSpec(memory_space=pltpu.SMEM), spec(0), spec(1), spec(2),
                  pl.BlockSpec((1, HEAD_DIM), lambda i, h: (0, 0))],
        out_specs=pl.BlockSpec((None, seq, HEAD_DIM), lambda i, h: (i, 0, h)),
        out_shape=jax.ShapeDtypeStruct((b, seq, heads * HEAD_DIM), BF16),
        compiler_params=_params("parallel", "parallel"),
        name="diff_attn",
    )(lam, proj, proj, proj, gain_g.reshape(1, HEAD_DIM))


def _sb_attn_kernel(q_ref, k_ref, v_ref, u_ref, o_ref, *, seq):
    blk = ATT_BLOCK
    n_full, tail = seq // blk, seq % blk
    scale = HEAD_DIM ** -0.5
    u2 = u_ref[...]

    def block_weights(z, mask, run):
        sp = jnp.maximum(z, 0.0) + jnp.log1p(jnp.exp(-jnp.abs(z)))
        log_keep = -sp if mask is None else jnp.where(mask, -sp, 0.0)
        hi = log_keep.astype(BF16)
        lo = (log_keep - hi.astype(F32)).astype(BF16)
        incl = _dot(jnp.concatenate([hi, lo], axis=1), u2)
        w = jnp.exp((z - sp) + (incl - log_keep) + run)
        if mask is not None:
            w = jnp.where(mask, w, 0.0)
        return w, run + incl[:, 0:1]

    def q_block(q0, t, n_prev):
        q = q_ref[pl.ds(q0, t), :]
        kd, vd = k_ref[pl.ds(q0, t), :], v_ref[pl.ds(q0, t), :]
        if t != blk:
            kd, vd = _pad_rows(kd, blk), _pad_rows(vd, blk)
        r, c = _block_iota(t, blk, t)
        w, run = block_weights(_dot_nt(q, kd) * scale, c < r, jnp.zeros((t, 1), F32))
        acc = _dot(w.astype(BF16), vd)

        def kv_step(jj, carry):
            run, acc = carry
            k0 = pl.multiple_of((n_prev - 1 - jj) * blk, blk)
            w, run = block_weights(_dot_nt(q, k_ref[pl.ds(k0, blk), :]) * scale, None, run)
            return run, acc + _dot(w.astype(BF16), v_ref[pl.ds(k0, blk), :])

        _, acc = lax.fori_loop(0, n_prev, kv_step, (run, acc))
        o_ref[pl.ds(q0, t), :] = acc.astype(o_ref.dtype)

    def full_block(i, _):
        q_block(pl.multiple_of(i * blk, blk), blk, i)
        return 0

    lax.fori_loop(0, n_full, full_block, 0)
    if tail:
        q_block(n_full * blk, tail, n_full)


def _sb_attention(proj, *, heads, col_off):
    b, seq, _ = proj.shape
    c0 = col_off // HEAD_DIM
    blk = ATT_BLOCK
    tri = (lax.broadcasted_iota(jnp.int32, (blk, blk), 0) >= lax.broadcasted_iota(jnp.int32, (blk, blk), 1))
    u2 = jnp.concatenate([tri, tri], axis=0).astype(BF16)
    spec = lambda s: pl.BlockSpec((None, seq, HEAD_DIM), lambda i, h: (i, 0, c0 + s * heads + h))
    return pl.pallas_call(
        functools.partial(_sb_attn_kernel, seq=seq),
        grid=(b, heads),
        in_specs=[spec(0), spec(1), spec(2), pl.BlockSpec((2 * blk, blk), lambda i, h: (0, 0))],
        out_specs=pl.BlockSpec((None, seq, HEAD_DIM), lambda i, h: (i, 0, h)),
        out_shape=jax.ShapeDtypeStruct((b, seq, heads * HEAD_DIM), BF16),
        compiler_params=_params("parallel", "parallel"),
        name="sb_attn",
    )(proj, proj, proj, u2)


def kernel(x, meta_tokens, emb_ln_g, emb_ln_b, w_in, short_conv_w, lambda_q1, lambda_k1, lambda_q2, lambda_k2,
           diff_norm_g, w_out, ln1_g, ln1_b, w_up, ffn_conv_w, w_down, ln2_g, ln2_b):
    bsz, _, d = x.shape
    depth = w_in.shape[0]
    conv_w = short_conv_w.shape[-1]
    d_ff = w_down.shape[1]
    diff_w = (w_in.shape[-1] - 3 * conv_w) // 6
    heads = diff_w // HEAD_DIM
    alpha = (2 * depth) ** 0.25

    meta = jnp.broadcast_to(meta_tokens[None].astype(x.dtype), (bsz, N_META, d))
    tokens = jnp.concatenate([meta, x], axis=1)
    seq = tokens.shape[1]
    rows = bsz * seq
    tm_seq = seq // 3
    tm_big = 2 * tm_seq
    tr = _pick_tile(rows, (192, 176, 96, 48, 16))
    tn_wide = _pick_tile(math.gcd(6 * diff_w, d), (512, 256, 128))
    tn_conv = _pick_tile(math.gcd(conv_w, d_ff), (256, 128))

    w_in_b = w_in.astype(BF16)
    w_up_b = w_up.astype(BF16)
    w_down_b = w_down.astype(BF16)
    w_out_c = w_out[:, :conv_w].astype(BF16)
    w_out_d = w_out[:, conv_w:conv_w + diff_w].astype(BF16)
    w_out_s = w_out[:, conv_w + diff_w:].astype(BF16)

    h, hb = _layer_norm(tokens.reshape(rows, d), emb_ln_g, emb_ln_b, tr=tr)
    for l in range(depth):
        lam_init = _lambda_init(l)
        lam = (jnp.exp(jnp.sum(lambda_q1[l] * lambda_k1[l])) - jnp.exp(jnp.sum(lambda_q2[l] * lambda_k2[l]))
               + lam_init).reshape(1).astype(F32)
        y_conv = _conv_mixer(hb, w_in_b, short_conv_w, l, width=conv_w, seq=seq, tm=tm_seq, tn=tn_conv)
        proj = _project(hb, w_in_b, l, col_off=3 * conv_w, n_cols=6 * diff_w, tm=tm_big, tn=tn_wide, out_dtype=BF16)
        proj = proj.reshape(bsz, seq, 6 * diff_w)
        y_diff = _diff_attention(proj, lam, diff_norm_g[l], heads=heads, col_off=0, lam_init=lam_init)
        y_sb = _sb_attention(proj, heads=heads, col_off=3 * diff_w)
        x1 = _out_proj(y_conv, y_diff.reshape(rows, diff_w), y_sb.reshape(rows, diff_w),
                       w_out_c, w_out_d, w_out_s, h, l, alpha=alpha, tm=tm_big, tn=tn_wide)
        h, hb = _layer_norm(x1, ln1_g[l], ln1_b[l], tr=tr)
        act = _ffn_up(hb, w_up_b, ffn_conv_w, l, d_ff=d_ff, seq=seq, tm=tm_seq, tn=tn_conv)
        x2 = _ffn_down(act, w_down_b, h, l, alpha=alpha, tm=tm_seq, tn=tn_conv)
        h, hb = _layer_norm(x2, ln2_g[l], ln2_b[l], tr=tr)
    return h.reshape(bsz, seq, d)[:, N_META:]
```

```python
import functools
import math

import jax
import jax.numpy as jnp
from jax import lax
from jax.experimental import pallas as pl
from jax.experimental.pallas import tpu as pltpu

N_META = 16
HEAD_DIM = 128
DIFF_QK_DIM = HEAD_DIM // 2
LN_EPS = 1e-5
ATT_BLOCK = 256
ATT_HEADS_PER_STEP = 4
CONV_HALO = 8
VMEM_LIMIT = 56 * 1024 * 1024

F32 = jnp.float32
BF16 = jnp.bfloat16


def _lambda_init(layer):
    return 0.8 - 0.6 * math.exp(-0.3 * layer)


def _pick_tile(n, candidates):
    return next(c for c in candidates if n % c == 0)


def _params(*sem):
    return pltpu.CompilerParams(dimension_semantics=sem, vmem_limit_bytes=VMEM_LIMIT)


def _dot(a, b):
    return jnp.dot(a, b, preferred_element_type=F32)


def _dot_nt(a, b):
    return lax.dot_general(a, b, (((1,), (1,)), ((), ())), preferred_element_type=F32)


def _ln_kernel(x_ref, g_ref, b_ref, of_ref, ob_ref):
    x = x_ref[...]
    mu = jnp.mean(x, axis=-1, keepdims=True)
    xc = x - mu
    var = jnp.mean(xc * xc, axis=-1, keepdims=True)
    y = xc * lax.rsqrt(var + LN_EPS) * g_ref[...] + b_ref[...]
    of_ref[...] = y
    ob_ref[...] = y.astype(BF16)


def _layer_norm(x, g, b, *, tr):
    rows, d = x.shape
    return pl.pallas_call(
        _ln_kernel,
        grid=(rows // tr,),
        in_specs=[pl.BlockSpec((tr, d), lambda i: (i, 0)),
                  pl.BlockSpec((1, d), lambda i: (0, 0)),
                  pl.BlockSpec((1, d), lambda i: (0, 0))],
        out_specs=[pl.BlockSpec((tr, d), lambda i: (i, 0)),
                   pl.BlockSpec((tr, d), lambda i: (i, 0))],
        out_shape=[jax.ShapeDtypeStruct((rows, d), F32), jax.ShapeDtypeStruct((rows, d), BF16)],
        compiler_params=_params("parallel"),
        name="layer_norm",
    )(x, g.reshape(1, d), b.reshape(1, d))


def _mm_kernel(x_ref, w_ref, o_ref):
    o_ref[...] = _dot(x_ref[...], w_ref[...]).astype(o_ref.dtype)


def _project(x, w, layer, *, col_off, n_cols, tm, tn, out_dtype):
    m, k = x.shape
    off = col_off // tn
    return pl.pallas_call(
        _mm_kernel,
        grid=(m // tm, n_cols // tn),
        in_specs=[pl.BlockSpec((tm, k), lambda i, j: (i, 0)),
                  pl.BlockSpec((None, k, tn), lambda i, j: (layer, 0, j + off))],
        out_specs=pl.BlockSpec((tm, tn), lambda i, j: (i, j)),
        out_shape=jax.ShapeDtypeStruct((m, n_cols), out_dtype),
        compiler_params=_params("parallel", "arbitrary"),
        name="proj_attn",
    )(x, w)


def _causal_conv3(u, w_ref, scr_ref, halo_ref, n, first_tile):
    tm = u.shape[0]

    @pl.when(first_tile)
    def _():
        halo_ref[n] = jnp.zeros(halo_ref.shape[1:], F32)

    scr_ref[0:CONV_HALO, :] = halo_ref[n]
    scr_ref[CONV_HALO:CONV_HALO + tm, :] = u
    halo_ref[n] = u[tm - CONV_HALO:tm, :]
    w = w_ref[...]
    return (w[0:1, :] * scr_ref[CONV_HALO - 2:CONV_HALO - 2 + tm, :]
            + w[1:2, :] * scr_ref[CONV_HALO - 1:CONV_HALO - 1 + tm, :]
            + w[2:3, :] * u)


def _conv_mixer_kernel(x_ref, wb_ref, wc_ref, wh_ref, cw_ref, o_ref, scr_ref, halo_ref, *, tiles_per_seq):
    i, n = pl.program_id(0), pl.program_id(1)
    x = x_ref[...]
    cb = _dot(x, wb_ref[...])
    g = _dot(x, wc_ref[...]) * _dot(x, wh_ref[...])
    y = _causal_conv3(g, cw_ref, scr_ref, halo_ref, n, i % tiles_per_seq == 0)
    o_ref[...] = (cb * y).astype(o_ref.dtype)


def _conv_mixer(x, w_in, conv_w, layer, *, width, seq, tm, tn):
    m, k = x.shape
    nt = width // tn
    kern = functools.partial(_conv_mixer_kernel, tiles_per_seq=seq // tm)
    wspec = lambda s: pl.BlockSpec((None, k, tn), lambda i, j: (layer, 0, j + s * nt))
    return pl.pallas_call(
        kern,
        grid=(m // tm, nt),
        in_specs=[pl.BlockSpec((tm, k), lambda i, j: (i, 0)), wspec(0), wspec(1), wspec(2),
                  pl.BlockSpec((None, 3, tn), lambda i, j: (layer, 0, j))],
        out_specs=pl.BlockSpec((tm, tn), lambda i, j: (i, j)),
        out_shape=jax.ShapeDtypeStruct((m, width), BF16),
        scratch_shapes=[pltpu.VMEM((CONV_HALO + tm, tn), F32), pltpu.VMEM((nt, CONV_HALO, tn), F32)],
        compiler_params=_params("arbitrary", "arbitrary"),
        name="conv_mixer",
    )(x, w_in, w_in, w_in, conv_w)


def _interleave_gate_up(w, tn):
    lead, d_ff = w.shape[:-1], w.shape[-1] // 2
    return w.reshape(*lead, 2, d_ff // tn, tn).swapaxes(-3, -2).reshape(*lead, 2 * d_ff)


def _ffn_up_kernel(x_ref, w_ref, cw_ref, o_ref, scr_ref, halo_ref, *, tiles_per_seq):
    i, n = pl.program_id(0), pl.program_id(1)
    tn = o_ref.shape[1]
    u = _dot(x_ref[...], w_ref[...])
    c = _causal_conv3(u, cw_ref, scr_ref, halo_ref, n, i % tiles_per_seq == 0)
    gate, up = c[:, :tn], c[:, tn:]
    o_ref[...] = (gate * jax.nn.sigmoid(gate) * up).astype(o_ref.dtype)


def _ffn_up(x, w_up, conv_w, layer, *, d_ff, seq, tm, tn):
    m, k = x.shape
    nt = d_ff // tn
    kern = functools.partial(_ffn_up_kernel, tiles_per_seq=seq // tm)
    return pl.pallas_call(
        kern,
        grid=(m // tm, nt),
        in_specs=[pl.BlockSpec((tm, k), lambda i, j: (i, 0)),
                  pl.BlockSpec((None, k, 2 * tn), lambda i, j: (layer, 0, j)),
                  pl.BlockSpec((None, 3, 2 * tn), lambda i, j: (layer, 0, j))],
        out_specs=pl.BlockSpec((tm, tn), lambda i, j: (i, j)),
        out_shape=jax.ShapeDtypeStruct((m, d_ff), BF16),
        scratch_shapes=[pltpu.VMEM((CONV_HALO + tm, 2 * tn), F32), pltpu.VMEM((nt, CONV_HALO, 2 * tn), F32)],
        compiler_params=_params("arbitrary", "arbitrary"),
        name="ffn_up",
    )(x, w_up, conv_w)


def _out_proj_kernel(yc_ref, yd_ref, ys_ref, wc_ref, wd_ref, ws_ref, h_ref, o_ref, *, alpha):
    mix = _dot(yc_ref[...], wc_ref[...]) + _dot(yd_ref[...], wd_ref[...]) + _dot(ys_ref[...], ws_ref[...])
    o_ref[...] = alpha * h_ref[...] + mix


def _out_proj(yc, yd, ys, wc, wd, ws, h, layer, *, alpha, tm, tn):
    m, d = h.shape
    lhs = lambda a: pl.BlockSpec((tm, a.shape[1]), lambda i, j: (i, 0))
    rhs = lambda w: pl.BlockSpec((None, w.shape[1], tn), lambda i, j: (layer, 0, j))
    return pl.pallas_call(
        functools.partial(_out_proj_kernel, alpha=alpha),
        grid=(m // tm, d // tn),
        in_specs=[lhs(yc), lhs(yd), lhs(ys), rhs(wc), rhs(wd), rhs(ws),
                  pl.BlockSpec((tm, tn), lambda i, j: (i, j))],
        out_specs=pl.BlockSpec((tm, tn), lambda i, j: (i, j)),
        out_shape=jax.ShapeDtypeStruct((m, d), F32),
        compiler_params=_params("parallel", "arbitrary"),
        name="out_proj",
    )(yc, yd, ys, wc, wd, ws, h)


def _mm_res_kernel(x_ref, w_ref, h_ref, o_ref, *, alpha):
    o_ref[...] = alpha * h_ref[...] + _dot(x_ref[...], w_ref[...])


def _ffn_down(a, w, h, layer, *, alpha, tm, tn):
    m, d = h.shape
    k = a.shape[1]
    return pl.pallas_call(
        functools.partial(_mm_res_kernel, alpha=alpha),
        grid=(m // tm, d // tn),
        in_specs=[pl.BlockSpec((tm, k), lambda i, j: (i, 0)),
                  pl.BlockSpec((None, k, tn), lambda i, j: (layer, 0, j)),
                  pl.BlockSpec((tm, tn), lambda i, j: (i, j))],
        out_specs=pl.BlockSpec((tm, tn), lambda i, j: (i, j)),
        out_shape=jax.ShapeDtypeStruct((m, d), F32),
        compiler_params=_params("parallel", "arbitrary"),
        name="ffn_down",
    )(a, w, h)


def _pad_rows(a, rows):
    return jnp.concatenate([a, jnp.zeros((rows - a.shape[0], a.shape[1]), a.dtype)], axis=0)


def _block_iota(rows, cols, t):
    r = lax.broadcasted_iota(jnp.int32, (rows, cols), 0)
    c = lax.broadcasted_iota(jnp.int32, (rows, cols), 1)
    return jnp.where(r >= t, r - t, r), c


def _head_cols(h):
    return slice(h * HEAD_DIM, (h + 1) * HEAD_DIM)


def _causal_sweep(seq, q_block):
    n_full, tail = seq // ATT_BLOCK, seq % ATT_BLOCK

    def full_block(i, _):
        q_block(pl.multiple_of(i * ATT_BLOCK, ATT_BLOCK), ATT_BLOCK, i)
        return 0

    lax.fori_loop(0, n_full, full_block, 0)
    if tail:
        q_block(n_full * ATT_BLOCK, tail, n_full)


def _diff_attn_kernel(lam_ref, q_ref, k_ref, v_ref, g_ref, o_ref, *, seq, gain, nh):
    blk = ATT_BLOCK
    lam = lam_ref[0]
    g = g_ref[...]
    lo_lane = lax.broadcasted_iota(jnp.int32, (1, HEAD_DIM), 1) < DIFF_QK_DIM

    def stacked_q(h, q0, t):
        q = (q_ref[pl.ds(q0, t), _head_cols(h)].astype(F32) * (DIFF_QK_DIM ** -0.5)).astype(BF16)
        zero = jnp.zeros_like(q)
        return jnp.concatenate([jnp.where(lo_lane, q, zero), jnp.where(lo_lane, zero, q)], axis=0)

    def keys(k0, t):
        ks = [k_ref[pl.ds(k0, t), _head_cols(h)] for h in range(nh)]
        return ks if t == blk else [_pad_rows(a, blk) for a in ks]

    def values_and_ones(k0, t):
        vs = [v_ref[pl.ds(k0, t), _head_cols(h)] for h in range(nh)]
        if t != blk:
            vs = [_pad_rows(a, blk) for a in vs]
        ones = jnp.ones((blk, HEAD_DIM), BF16)
        return [jnp.concatenate([v, ones], axis=1) for v in vs]

    def scores(qs, ks, mask):
        ss = [_dot_nt(q, k) for q, k in zip(qs, ks)]
        return ss if mask is None else [jnp.where(mask, s, -jnp.inf) for s in ss]

    def lane_tile_max(s):
        tiles = [s[:, i:i + HEAD_DIM] for i in range(0, blk, HEAD_DIM)]
        return functools.reduce(jnp.maximum, tiles)

    def q_block(q0, t, n_prev):
        heads = range(nh)
        qs = [stacked_q(h, q0, t) for h in heads]
        r, c = _block_iota(2 * t, blk, t)
        causal = c <= r
        kd = keys(q0, t)

        def max_step(j, mx):
            ss = scores(qs, keys(pl.multiple_of(j * blk, blk), blk), None)
            return tuple(jnp.maximum(mx[h], lane_tile_max(ss[h])) for h in heads)

        mx = tuple(lane_tile_max(s) for s in scores(qs, kd, causal))
        mx = lax.fori_loop(0, n_prev, max_step, mx)
        ms = [jnp.max(m, axis=-1, keepdims=True) for m in mx]

        def weighted(ss, vs):
            return [_dot(jnp.exp(ss[h] - ms[h]).astype(BF16), vs[h]) for h in heads]

        def sum_step(j, accs):
            k0 = pl.multiple_of(j * blk, blk)
            pvs = weighted(scores(qs, keys(k0, blk), None), values_and_ones(k0, blk))
            return tuple(accs[h] + pvs[h] for h in heads)

        accs = lax.fori_loop(0, n_prev, sum_step, tuple(weighted(scores(qs, kd, causal), values_and_ones(q0, t))))
        for h in heads:
            o = accs[h][:, :HEAD_DIM] / accs[h][:, HEAD_DIM:]
            o = o[:t] - lam * o[t:]
            o = o * lax.rsqrt(jnp.mean(o * o, axis=-1, keepdims=True) + LN_EPS)
            o_ref[pl.ds(q0, t), _head_cols(h)] = (o * g * gain).astype(o_ref.dtype)

    _causal_sweep(seq, q_block)


def _diff_attention(proj, lam, gain_g, *, heads, col_off, lam_init, nh):
    b, seq, _ = proj.shape
    w = nh * HEAD_DIM
    c0 = col_off // w
    spec = lambda s: pl.BlockSpec((None, seq, w), lambda i, h: (i, 0, c0 + s * (heads // nh) + h))
    return pl.pallas_call(
        functools.partial(_diff_attn_kernel, seq=seq, gain=1.0 - lam_init, nh=nh),
        grid=(b, heads // nh),
        in_specs=[pl.BlockSpec(memory_space=pltpu.SMEM), spec(0), spec(1), spec(2),
                  pl.BlockSpec((1, HEAD_DIM), lambda i, h: (0, 0))],
        out_specs=pl.BlockSpec((None, seq, w), lambda i, h: (i, 0, h)),
        out_shape=jax.ShapeDtypeStruct((b, seq, heads * HEAD_DIM), BF16),
        compiler_params=_params("parallel", "parallel"),
        name="diff_attn",
    )(lam, proj, proj, proj, gain_g.reshape(1, HEAD_DIM))


def _sb_attn_kernel(q_ref, k_ref, v_ref, u_ref, o_ref, *, seq, nh):
    blk = ATT_BLOCK
    scale = HEAD_DIM ** -0.5

    def block_step(qs, ks, vs, mask, runs, accs):
        heads = range(nh)
        zs = [_dot_nt(qs[h], ks[h]) * scale for h in heads]
        sps = [jnp.maximum(z, 0.0) + jnp.log(1.0 + jnp.exp(-jnp.abs(z))) for z in zs]
        log_keeps = [-sp if mask is None else jnp.where(mask, -sp, 0.0) for sp in sps]
        his = [lk.astype(BF16) for lk in log_keeps]
        los = [(lk - hi.astype(F32)).astype(BF16) for lk, hi in zip(log_keeps, his)]
        incls = [_dot(jnp.concatenate([his[h], los[h]], axis=1), u_ref[...]) for h in heads]
        ws = [jnp.exp(zs[h] + incls[h] + runs[h]) for h in heads]
        if mask is not None:
            ws = [jnp.where(mask, w, 0.0) for w in ws]
        pvs = [_dot(ws[h].astype(BF16), vs[h]) for h in heads]
        runs = [runs[h] + jnp.sum(log_keeps[h], axis=-1, keepdims=True) for h in heads]
        accs = [pvs[h] if accs is None else accs[h] + pvs[h] for h in heads]
        return runs, accs

    def q_block(q0, t, n_prev):
        qs = [q_ref[pl.ds(q0, t), _head_cols(h)] for h in range(nh)]
        r, c = _block_iota(t, blk, t)
        kd = [k_ref[pl.ds(q0, t), _head_cols(h)] for h in range(nh)]
        vd = [v_ref[pl.ds(q0, t), _head_cols(h)] for h in range(nh)]
        if t != blk:
            kd, vd = [_pad_rows(a, blk) for a in kd], [_pad_rows(a, blk) for a in vd]
        runs, accs = block_step(qs, kd, vd, c < r, [jnp.zeros((t, 1), F32)] * nh, None)

        def kv_step(jj, carry):
            k0 = pl.multiple_of((n_prev - 1 - jj) * blk, blk)
            ks = [k_ref[pl.ds(k0, blk), _head_cols(h)] for h in range(nh)]
            vs = [v_ref[pl.ds(k0, blk), _head_cols(h)] for h in range(nh)]
            runs, accs = block_step(qs, ks, vs, None, carry[:nh], carry[nh:])
            return tuple(runs) + tuple(accs)

        carry = lax.fori_loop(0, n_prev, kv_step, tuple(runs) + tuple(accs))
        for h in range(nh):
            o_ref[pl.ds(q0, t), _head_cols(h)] = carry[nh + h].astype(o_ref.dtype)

    _causal_sweep(seq, q_block)


def _sb_attention(proj, *, heads, col_off, nh):
    b, seq, _ = proj.shape
    w = nh * HEAD_DIM
    c0 = col_off // w
    blk = ATT_BLOCK
    tri = (lax.broadcasted_iota(jnp.int32, (blk, blk), 0) >= lax.broadcasted_iota(jnp.int32, (blk, blk), 1))
    u2 = jnp.concatenate([tri, tri], axis=0).astype(BF16)
    spec = lambda s: pl.BlockSpec((None, seq, w), lambda i, h: (i, 0, c0 + s * (heads // nh) + h))
    return pl.pallas_call(
        functools.partial(_sb_attn_kernel, seq=seq, nh=nh),
        grid=(b, heads // nh),
        in_specs=[spec(0), spec(1), spec(2), pl.BlockSpec((2 * blk, blk), lambda i, h: (0, 0))],
        out_specs=pl.BlockSpec((None, seq, w), lambda i, h: (i, 0, h)),
        out_shape=jax.ShapeDtypeStruct((b, seq, heads * HEAD_DIM), BF16),
        compiler_params=_params("parallel", "parallel"),
        name="sb_attn",
    )(proj, proj, proj, u2)


def kernel(x, meta_tokens, emb_ln_g, emb_ln_b, w_in, short_conv_w, lambda_q1, lambda_k1, lambda_q2, lambda_k2,
           diff_norm_g, w_out, ln1_g, ln1_b, w_up, ffn_conv_w, w_down, ln2_g, ln2_b):
    bsz, _, d = x.shape
    depth = w_in.shape[0]
    conv_w = short_conv_w.shape[-1]
    d_ff = w_down.shape[1]
    diff_w = (w_in.shape[-1] - 3 * conv_w) // 6
    heads = diff_w // HEAD_DIM
    alpha = (2 * depth) ** 0.25

    meta = jnp.broadcast_to(meta_tokens[None].astype(x.dtype), (bsz, N_META, d))
    tokens = jnp.concatenate([meta, x], axis=1)
    seq = tokens.shape[1]
    rows = bsz * seq
    tm_seq = seq // 3
    tm_big = 2 * tm_seq
    tr = _pick_tile(rows, (192, 176, 96, 48, 16))
    tn_wide = _pick_tile(math.gcd(6 * diff_w, d), (512, 256, 128))
    tn_conv = _pick_tile(math.gcd(conv_w, d_ff), (256, 128))
    nh = _pick_tile(heads, (ATT_HEADS_PER_STEP, 1))

    w_in_b = w_in.astype(BF16)
    w_up_b = _interleave_gate_up(w_up, tn_conv).astype(BF16)
    ffn_conv_i = _interleave_gate_up(ffn_conv_w, tn_conv)
    w_down_b = w_down.astype(BF16)
    w_out_c = w_out[:, :conv_w].astype(BF16)
    w_out_d = w_out[:, conv_w:conv_w + diff_w].astype(BF16)
    w_out_s = w_out[:, conv_w + diff_w:].astype(BF16)

    h, hb = _layer_norm(tokens.reshape(rows, d), emb_ln_g, emb_ln_b, tr=tr)
    for l in range(depth):
        lam_init = _lambda_init(l)
        lam = (jnp.exp(jnp.sum(lambda_q1[l] * lambda_k1[l])) - jnp.exp(jnp.sum(lambda_q2[l] * lambda_k2[l]))
               + lam_init).reshape(1).astype(F32)
        y_conv = _conv_mixer(hb, w_in_b, short_conv_w, l, width=conv_w, seq=seq, tm=tm_seq, tn=tn_conv)
        proj = _project(hb, w_in_b, l, col_off=3 * conv_w, n_cols=6 * diff_w, tm=tm_big, tn=tn_wide, out_dtype=BF16)
        proj = proj.reshape(bsz, seq, 6 * diff_w)
        y_diff = _diff_attention(proj, lam, diff_norm_g[l], heads=heads, col_off=0, lam_init=lam_init, nh=nh)
        y_sb = _sb_attention(proj, heads=heads, col_off=3 * diff_w, nh=nh)
        x1 = _out_proj(y_conv, y_diff.reshape(rows, diff_w), y_sb.reshape(rows, diff_w),
                       w_out_c, w_out_d, w_out_s, h, l, alpha=alpha, tm=tm_big, tn=tn_wide)
        h, hb = _layer_norm(x1, ln1_g[l], ln1_b[l], tr=tr)
        act = _ffn_up(hb, w_up_b, ffn_conv_i, l, d_ff=d_ff, seq=seq, tm=tm_seq, tn=tn_conv)
        x2 = _ffn_down(act, w_down_b, h, l, alpha=alpha, tm=tm_seq, tn=tn_conv)
        h, hb = _layer_norm(x2, ln2_g[l], ln2_b[l], tr=tr)
    return h.reshape(bsz, seq, d)[:, N_META:]
```

```python
import functools
import math

import jax
import jax.numpy as jnp
from jax import lax
from jax.experimental import pallas as pl
from jax.experimental.pallas import tpu as pltpu

N_META = 16
HEAD_DIM = 128
DIFF_QK_DIM = HEAD_DIM // 2
LN_EPS = 1e-5
ATT_BLOCK = 256
ATT_HEADS_PER_STEP = 4
CONV_HALO = 8
VMEM_LIMIT = 56 * 1024 * 1024

F32 = jnp.float32
BF16 = jnp.bfloat16


def _lambda_init(layer):
    return 0.8 - 0.6 * math.exp(-0.3 * layer)


def _pick_tile(n, candidates):
    return next(c for c in candidates if n % c == 0)


def _params(*sem):
    return pltpu.CompilerParams(dimension_semantics=sem, vmem_limit_bytes=VMEM_LIMIT)


def _dot(a, b):
    return jnp.dot(a, b, preferred_element_type=F32)


def _dot_nt(a, b):
    return lax.dot_general(a, b, (((1,), (1,)), ((), ())), preferred_element_type=F32)


def _ln_kernel(x_ref, g_ref, b_ref, of_ref, ob_ref):
    x = x_ref[...]
    mu = jnp.mean(x, axis=-1, keepdims=True)
    xc = x - mu
    var = jnp.mean(xc * xc, axis=-1, keepdims=True)
    y = xc * lax.rsqrt(var + LN_EPS) * g_ref[...] + b_ref[...]
    of_ref[...] = y
    ob_ref[...] = y.astype(BF16)


def _layer_norm(x, g, b, *, tr):
    rows, d = x.shape
    return pl.pallas_call(
        _ln_kernel,
        grid=(rows // tr,),
        in_specs=[pl.BlockSpec((tr, d), lambda i: (i, 0)),
                  pl.BlockSpec((1, d), lambda i: (0, 0)),
                  pl.BlockSpec((1, d), lambda i: (0, 0))],
        out_specs=[pl.BlockSpec((tr, d), lambda i: (i, 0)),
                   pl.BlockSpec((tr, d), lambda i: (i, 0))],
        out_shape=[jax.ShapeDtypeStruct((rows, d), F32), jax.ShapeDtypeStruct((rows, d), BF16)],
        compiler_params=_params("parallel"),
        name="layer_norm",
    )(x, g.reshape(1, d), b.reshape(1, d))


def _mm_kernel(x_ref, w_ref, o_ref):
    o_ref[...] = _dot(x_ref[...], w_ref[...]).astype(o_ref.dtype)


def _project(x, w, layer, *, col_off, n_cols, tm, tn, out_dtype):
    m, k = x.shape
    off = col_off // tn
    return pl.pallas_call(
        _mm_kernel,
        grid=(m // tm, n_cols // tn),
        in_specs=[pl.BlockSpec((tm, k), lambda i, j: (i, 0)),
                  pl.BlockSpec((None, k, tn), lambda i, j: (layer, 0, j + off))],
        out_specs=pl.BlockSpec((tm, tn), lambda i, j: (i, j)),
        out_shape=jax.ShapeDtypeStruct((m, n_cols), out_dtype),
        compiler_params=_params("parallel", "arbitrary"),
        name="proj_attn",
    )(x, w)


def _causal_conv3(u, w, scr_ref, halo_ref, first_tile):
    tm = u.shape[0]
    scr_ref[0:CONV_HALO, :] = jnp.where(first_tile, 0.0, halo_ref[...])
    scr_ref[CONV_HALO:CONV_HALO + tm, :] = u
    halo_ref[...] = u[tm - CONV_HALO:tm, :]
    return (w[0:1, :] * scr_ref[CONV_HALO - 2:CONV_HALO - 2 + tm, :]
            + w[1:2, :] * scr_ref[CONV_HALO - 1:CONV_HALO - 1 + tm, :]
            + w[2:3, :] * u)


def _conv_mixer_kernel(x_ref, wb_ref, wc_ref, wh_ref, cw_ref, o_ref, scr_ref, halo_ref, *, tiles_per_seq):
    i, n = pl.program_id(0), pl.program_id(1)

    @pl.when(i == 0)
    def _():
        halo_ref[n] = jnp.zeros(halo_ref.shape[1:], F32)

    x = x_ref[...]
    cb = _dot(x, wb_ref[...])
    g = _dot(x, wc_ref[...]) * _dot(x, wh_ref[...])
    y = _causal_conv3(g, cw_ref[...], scr_ref, halo_ref.at[n], i % tiles_per_seq == 0)
    o_ref[...] = (cb * y).astype(o_ref.dtype)


def _conv_mixer(x, w_in, conv_w, layer, *, width, seq, tm, tn):
    m, k = x.shape
    nt = width // tn
    kern = functools.partial(_conv_mixer_kernel, tiles_per_seq=seq // tm)
    wspec = lambda s: pl.BlockSpec((None, k, tn), lambda i, j: (layer, 0, j + s * nt))
    return pl.pallas_call(
        kern,
        grid=(m // tm, nt),
        in_specs=[pl.BlockSpec((tm, k), lambda i, j: (i, 0)), wspec(0), wspec(1), wspec(2),
                  pl.BlockSpec((None, 3, tn), lambda i, j: (layer, 0, j))],
        out_specs=pl.BlockSpec((tm, tn), lambda i, j: (i, j)),
        out_shape=jax.ShapeDtypeStruct((m, width), BF16),
        scratch_shapes=[pltpu.VMEM((CONV_HALO + tm, tn), F32), pltpu.VMEM((nt, CONV_HALO, tn), F32)],
        compiler_params=_params("arbitrary", "arbitrary"),
        name="conv_mixer",
    )(x, w_in, w_in, w_in, conv_w)


def _ffn_up_kernel(x_ref, wg_ref, wu_ref, cg_ref, cu_ref, o_ref, w_scr, scr_ref, halo_ref, *, sub, subs_per_seq):
    i = pl.program_id(1)
    tn = o_ref.shape[1]
    ts = o_ref.shape[0] // sub

    @pl.when(i == 0)
    def _():
        w_scr[:, :tn] = wg_ref[...].astype(BF16)
        w_scr[:, tn:] = wu_ref[...].astype(BF16)
        halo_ref[...] = jnp.zeros(halo_ref.shape, F32)

    cw = jnp.concatenate([cg_ref[...], cu_ref[...]], axis=1)
    for s in range(sub):
        rows = slice(s * ts, (s + 1) * ts)
        u = _dot(x_ref[rows, :], w_scr[...])
        c = _causal_conv3(u, cw, scr_ref.at[s], halo_ref, (i * sub + s) % subs_per_seq == 0)
        gate, up = c[:, :tn], c[:, tn:]
        o_ref[rows, :] = (gate * jax.nn.sigmoid(gate) * up).astype(o_ref.dtype)


def _ffn_up(x, w_up, conv_w, layer, *, d_ff, seq, tm, tn, sub):
    m, k = x.shape
    nt = d_ff // tn
    kern = functools.partial(_ffn_up_kernel, sub=sub, subs_per_seq=seq * sub // tm)
    return pl.pallas_call(
        kern,
        grid=(nt, m // tm),
        in_specs=[pl.BlockSpec((tm, k), lambda j, i: (i, 0)),
                  pl.BlockSpec((None, k, tn), lambda j, i: (layer, 0, j)),
                  pl.BlockSpec((None, k, tn), lambda j, i: (layer, 0, j + nt)),
                  pl.BlockSpec((None, 3, tn), lambda j, i: (layer, 0, j)),
                  pl.BlockSpec((None, 3, tn), lambda j, i: (layer, 0, j + nt))],
        out_specs=pl.BlockSpec((tm, tn), lambda j, i: (i, j)),
        out_shape=jax.ShapeDtypeStruct((m, d_ff), BF16),
        scratch_shapes=[pltpu.VMEM((k, 2 * tn), BF16), pltpu.VMEM((sub, CONV_HALO + tm // sub, 2 * tn), F32),
                        pltpu.VMEM((CONV_HALO, 2 * tn), F32)],
        compiler_params=_params("arbitrary", "arbitrary"),
        name="ffn_up",
    )(x, w_up, w_up, conv_w, conv_w)


def _out_proj_kernel(yc_ref, yd_ref, ys_ref, wc_ref, wd_ref, ws_ref, h_ref, o_ref, *, alpha):
    mix = _dot(yc_ref[...], wc_ref[...]) + _dot(yd_ref[...], wd_ref[...]) + _dot(ys_ref[...], ws_ref[...])
    o_ref[...] = alpha * h_ref[...] + mix


def _out_proj(yc, yd, ys, wc, wd, ws, h, layer, *, alpha, tm, tn):
    m, d = h.shape
    lhs = lambda a: pl.BlockSpec((tm, a.shape[1]), lambda i, j: (i, 0))
    rhs = lambda w: pl.BlockSpec((None, w.shape[1], tn), lambda i, j: (layer, 0, j))
    return pl.pallas_call(
        functools.partial(_out_proj_kernel, alpha=alpha),
        grid=(m // tm, d // tn),
        in_specs=[lhs(yc), lhs(yd), lhs(ys), rhs(wc), rhs(wd), rhs(ws),
                  pl.BlockSpec((tm, tn), lambda i, j: (i, j))],
        out_specs=pl.BlockSpec((tm, tn), lambda i, j: (i, j)),
        out_shape=jax.ShapeDtypeStruct((m, d), F32),
        compiler_params=_params("parallel", "arbitrary"),
        name="out_proj",
    )(yc, yd, ys, wc, wd, ws, h)


def _mm_res_kernel(x_ref, w_ref, h_ref, o_ref, *, alpha):
    o_ref[...] = alpha * h_ref[...] + _dot(x_ref[...], w_ref[...])


def _ffn_down(a, w, h, layer, *, alpha, tm, tn):
    m, d = h.shape
    k = a.shape[1]
    return pl.pallas_call(
        functools.partial(_mm_res_kernel, alpha=alpha),
        grid=(m // tm, d // tn),
        in_specs=[pl.BlockSpec((tm, k), lambda i, j: (i, 0)),
                  pl.BlockSpec((None, k, tn), lambda i, j: (layer, 0, j)),
                  pl.BlockSpec((tm, tn), lambda i, j: (i, j))],
        out_specs=pl.BlockSpec((tm, tn), lambda i, j: (i, j)),
        out_shape=jax.ShapeDtypeStruct((m, d), F32),
        compiler_params=_params("parallel", "arbitrary"),
        name="ffn_down",
    )(a, w, h)


def _pad_rows(a, rows):
    return jnp.concatenate([a, jnp.zeros((rows - a.shape[0], a.shape[1]), a.dtype)], axis=0)


def _block_iota(rows, cols, t):
    r = lax.broadcasted_iota(jnp.int32, (rows, cols), 0)
    c = lax.broadcasted_iota(jnp.int32, (rows, cols), 1)
    return jnp.where(r >= t, r - t, r), c


def _head_cols(h):
    return slice(h * HEAD_DIM, (h + 1) * HEAD_DIM)


def _causal_sweep(seq, q_block):
    n_full, tail = seq // ATT_BLOCK, seq % ATT_BLOCK

    def full_block(i, _):
        q_block(pl.multiple_of(i * ATT_BLOCK, ATT_BLOCK), ATT_BLOCK, i)
        return 0

    lax.fori_loop(0, n_full, full_block, 0)
    if tail:
        q_block(n_full * ATT_BLOCK, tail, n_full)


def _diff_attn_kernel(lam_ref, q_ref, k_ref, v_ref, g_ref, o_ref, *, seq, gain, nh):
    blk = ATT_BLOCK
    lam = lam_ref[0]
    g = g_ref[...]
    lo_lane = lax.broadcasted_iota(jnp.int32, (1, HEAD_DIM), 1) < DIFF_QK_DIM

    def stacked_q(h, q0, t):
        q = (q_ref[pl.ds(q0, t), _head_cols(h)].astype(F32) * (DIFF_QK_DIM ** -0.5)).astype(BF16)
        zero = jnp.zeros_like(q)
        return jnp.concatenate([jnp.where(lo_lane, q, zero), jnp.where(lo_lane, zero, q)], axis=0)

    def keys(k0, t):
        ks = [k_ref[pl.ds(k0, t), _head_cols(h)] for h in range(nh)]
        return ks if t == blk else [_pad_rows(a, blk) for a in ks]

    def values_and_ones(k0, t):
        vs = [v_ref[pl.ds(k0, t), _head_cols(h)] for h in range(nh)]
        if t != blk:
            vs = [_pad_rows(a, blk) for a in vs]
        ones = jnp.ones((blk, HEAD_DIM), BF16)
        return [jnp.concatenate([v, ones], axis=1) for v in vs]

    def scores(qs, ks, mask):
        ss = [_dot_nt(q, k) for q, k in zip(qs, ks)]
        return ss if mask is None else [jnp.where(mask, s, -jnp.inf) for s in ss]

    def lane_tile_max(s):
        tiles = [s[:, i:i + HEAD_DIM] for i in range(0, blk, HEAD_DIM)]
        return functools.reduce(jnp.maximum, tiles)

    def q_block(q0, t, n_prev):
        heads = range(nh)
        qs = [stacked_q(h, q0, t) for h in heads]
        r, c = _block_iota(2 * t, blk, t)
        causal = c <= r
        kd = keys(q0, t)

        def max_step(j, mx):
            ss = scores(qs, keys(pl.multiple_of(j * blk, blk), blk), None)
            return tuple(jnp.maximum(mx[h], lane_tile_max(ss[h])) for h in heads)

        mx = tuple(lane_tile_max(s) for s in scores(qs, kd, causal))
        mx = lax.fori_loop(0, n_prev, max_step, mx)
        ms = [jnp.max(m, axis=-1, keepdims=True) for m in mx]

        def weighted(ss, vs):
            return [_dot(jnp.exp(ss[h] - ms[h]).astype(BF16), vs[h]) for h in heads]

        def sum_step(j, accs):
            k0 = pl.multiple_of(j * blk, blk)
            pvs = weighted(scores(qs, keys(k0, blk), None), values_and_ones(k0, blk))
            return tuple(accs[h] + pvs[h] for h in heads)

        accs = lax.fori_loop(0, n_prev, sum_step, tuple(weighted(scores(qs, kd, causal), values_and_ones(q0, t))))
        for h in heads:
            o = accs[h][:, :HEAD_DIM] / accs[h][:, HEAD_DIM:]
            o = o[:t] - lam * o[t:]
            o = o * lax.rsqrt(jnp.mean(o * o, axis=-1, keepdims=True) + LN_EPS)
            o_ref[pl.ds(q0, t), _head_cols(h)] = (o * g * gain).astype(o_ref.dtype)

    _causal_sweep(seq, q_block)


def _diff_attention(proj, lam, gain_g, *, heads, col_off, lam_init, nh):
    b, seq, _ = proj.shape
    w = nh * HEAD_DIM
    c0 = col_off // w
    spec = lambda s: pl.BlockSpec((None, seq, w), lambda i, h: (i, 0, c0 + s * (heads // nh) + h))
    return pl.pallas_call(
        functools.partial(_diff_attn_kernel, seq=seq, gain=1.0 - lam_init, nh=nh),
        grid=(b, heads // nh),
        in_specs=[pl.BlockSpec(memory_space=pltpu.SMEM), spec(0), spec(1), spec(2),
                  pl.BlockSpec((1, HEAD_DIM), lambda i, h: (0, 0))],
        out_specs=pl.BlockSpec((None, seq, w), lambda i, h: (i, 0, h)),
        out_shape=jax.ShapeDtypeStruct((b, seq, heads * HEAD_DIM), BF16),
        compiler_params=_params("parallel", "parallel"),
        name="diff_attn",
    )(lam, proj, proj, proj, gain_g.reshape(1, HEAD_DIM))


def _sb_attn_kernel(q_ref, k_ref, v_ref, u_ref, o_ref, *, seq, nh):
    blk = ATT_BLOCK
    scale = HEAD_DIM ** -0.5

    def block_step(qs, ks, vs, mask, runs, accs):
        heads = range(nh)
        zs = [_dot_nt(qs[h], ks[h]) * scale for h in heads]
        sps = [jnp.maximum(z, 0.0) + jnp.log(1.0 + jnp.exp(-jnp.abs(z))) for z in zs]
        log_keeps = [-sp if mask is None else jnp.where(mask, -sp, 0.0) for sp in sps]
        his = [lk.astype(BF16) for lk in log_keeps]
        los = [(lk - hi.astype(F32)).astype(BF16) for lk, hi in zip(log_keeps, his)]
        incls = [_dot(jnp.concatenate([his[h], los[h]], axis=1), u_ref[...]) for h in heads]
        ws = [jnp.exp(zs[h] + incls[h] + runs[h]) for h in heads]
        if mask is not None:
            ws = [jnp.where(mask, w, 0.0) for w in ws]
        pvs = [_dot(ws[h].astype(BF16), vs[h]) for h in heads]
        runs = [runs[h] + jnp.sum(log_keeps[h], axis=-1, keepdims=True) for h in heads]
        accs = [pvs[h] if accs is None else accs[h] + pvs[h] for h in heads]
        return runs, accs

    def q_block(q0, t, n_prev):
        qs = [q_ref[pl.ds(q0, t), _head_cols(h)] for h in range(nh)]
        r, c = _block_iota(t, blk, t)
        kd = [k_ref[pl.ds(q0, t), _head_cols(h)] for h in range(nh)]
        vd = [v_ref[pl.ds(q0, t), _head_cols(h)] for h in range(nh)]
        if t != blk:
            kd, vd = [_pad_rows(a, blk) for a in kd], [_pad_rows(a, blk) for a in vd]
        runs, accs = block_step(qs, kd, vd, c < r, [jnp.zeros((t, 1), F32)] * nh, None)

        def kv_step(jj, carry):
            k0 = pl.multiple_of((n_prev - 1 - jj) * blk, blk)
            ks = [k_ref[pl.ds(k0, blk), _head_cols(h)] for h in range(nh)]
            vs = [v_ref[pl.ds(k0, blk), _head_cols(h)] for h in range(nh)]
            runs, accs = block_step(qs, ks, vs, None, carry[:nh], carry[nh:])
            return tuple(runs) + tuple(accs)

        carry = lax.fori_loop(0, n_prev, kv_step, tuple(runs) + tuple(accs))
        for h in range(nh):
            o_ref[pl.ds(q0, t), _head_cols(h)] = carry[nh + h].astype(o_ref.dtype)

    _causal_sweep(seq, q_block)


def _sb_attention(proj, *, heads, col_off, nh):
    b, seq, _ = proj.shape
    w = nh * HEAD_DIM
    c0 = col_off // w
    blk = ATT_BLOCK
    tri = (lax.broadcasted_iota(jnp.int32, (blk, blk), 0) >= lax.broadcasted_iota(jnp.int32, (blk, blk), 1))
    u2 = jnp.concatenate([tri, tri], axis=0).astype(BF16)
    spec = lambda s: pl.BlockSpec((None, seq, w), lambda i, h: (i, 0, c0 + s * (heads // nh) + h))
    return pl.pallas_call(
        functools.partial(_sb_attn_kernel, seq=seq, nh=nh),
        grid=(b, heads // nh),
        in_specs=[spec(0), spec(1), spec(2), pl.BlockSpec((2 * blk, blk), lambda i, h: (0, 0))],
        out_specs=pl.BlockSpec((None, seq, w), lambda i, h: (i, 0, h)),
        out_shape=jax.ShapeDtypeStruct((b, seq, heads * HEAD_DIM), BF16),
        compiler_params=_params("parallel", "parallel"),
        name="sb_attn",
    )(proj, proj, proj, u2)


def kernel(x, meta_tokens, emb_ln_g, emb_ln_b, w_in, short_conv_w, lambda_q1, lambda_k1, lambda_q2, lambda_k2,
           diff_norm_g, w_out, ln1_g, ln1_b, w_up, ffn_conv_w, w_down, ln2_g, ln2_b):
    bsz, _, d = x.shape
    depth = w_in.shape[0]
    conv_w = short_conv_w.shape[-1]
    d_ff = w_down.shape[1]
    diff_w = (w_in.shape[-1] - 3 * conv_w) // 6
    heads = diff_w // HEAD_DIM
    alpha = (2 * depth) ** 0.25

    meta = jnp.broadcast_to(meta_tokens[None].astype(x.dtype), (bsz, N_META, d))
    tokens = jnp.concatenate([meta, x], axis=1)
    seq = tokens.shape[1]
    rows = bsz * seq
    tm_seq = seq // 3
    tm_big = 2 * tm_seq
    tr = _pick_tile(rows, (192, 176, 96, 48, 16))
    tn_wide = _pick_tile(math.gcd(6 * diff_w, d), (512, 256, 128))
    tn_conv = _pick_tile(math.gcd(conv_w, d_ff), (256, 128))
    nh = _pick_tile(heads, (ATT_HEADS_PER_STEP, 1))

    w_in_b = w_in.astype(BF16)
    w_down_b = w_down.astype(BF16)
    w_out_c = w_out[:, :conv_w].astype(BF16)
    w_out_d = w_out[:, conv_w:conv_w + diff_w].astype(BF16)
    w_out_s = w_out[:, conv_w + diff_w:].astype(BF16)

    h, hb = _layer_norm(tokens.reshape(rows, d), emb_ln_g, emb_ln_b, tr=tr)
    for l in range(depth):
        lam_init = _lambda_init(l)
        lam = (jnp.exp(jnp.sum(lambda_q1[l] * lambda_k1[l])) - jnp.exp(jnp.sum(lambda_q2[l] * lambda_k2[l]))
               + lam_init).reshape(1).astype(F32)
        y_conv = _conv_mixer(hb, w_in_b, short_conv_w, l, width=conv_w, seq=seq, tm=tm_seq, tn=tn_conv)
        proj = _project(hb, w_in_b, l, col_off=3 * conv_w, n_cols=6 * diff_w, tm=tm_big, tn=tn_wide, out_dtype=BF16)
        proj = proj.reshape(bsz, seq, 6 * diff_w)
        y_diff = _diff_attention(proj, lam, diff_norm_g[l], heads=heads, col_off=0, lam_init=lam_init, nh=nh)
        y_sb = _sb_attention(proj, heads=heads, col_off=3 * diff_w, nh=nh)
        x1 = _out_proj(y_conv, y_diff.reshape(rows, diff_w), y_sb.reshape(rows, diff_w),
                       w_out_c, w_out_d, w_out_s, h, l, alpha=alpha, tm=tm_big, tn=tn_wide)
        h, hb = _layer_norm(x1, ln1_g[l], ln1_b[l], tr=tr)
        act = _ffn_up(hb, w_up, ffn_conv_w, l, d_ff=d_ff, seq=seq, tm=tm_big, tn=tn_conv, sub=2)
        x2 = _ffn_down(act, w_down_b, h, l, alpha=alpha, tm=tm_seq, tn=tn_conv)
        h, hb = _layer_norm(x2, ln2_g[l], ln2_b[l], tr=tr)
    return h.reshape(bsz, seq, d)[:, N_META:]
```

```python
import functools
import math

import jax
import jax.numpy as jnp
from jax import lax
from jax.experimental import pallas as pl
from jax.experimental.pallas import tpu as pltpu

N_META = 16
HEAD_DIM = 128
DIFF_QK_DIM = HEAD_DIM // 2
LN_EPS = 1e-5
ATT_BLOCK = 256
ATT_HEADS_PER_STEP = 4
ATT_TAIL_LANES = 128
ONES_ROWS = 16
CONV_HALO = 8
VMEM_LIMIT = 56 * 1024 * 1024

F32 = jnp.float32
BF16 = jnp.bfloat16


def _lambda_init(layer):
    return 0.8 - 0.6 * math.exp(-0.3 * layer)


def _pick_tile(n, candidates):
    return next(c for c in candidates if n % c == 0)


def _params(*sem):
    return pltpu.CompilerParams(dimension_semantics=sem, vmem_limit_bytes=VMEM_LIMIT)


def _dot(a, b):
    return jnp.dot(a, b, preferred_element_type=F32)


def _dot_nt(a, b):
    return lax.dot_general(a, b, (((1,), (1,)), ((), ())), preferred_element_type=F32)


def _ln_kernel(x_ref, g_ref, b_ref, of_ref, ob_ref):
    x = x_ref[...]
    mu = jnp.mean(x, axis=-1, keepdims=True)
    xc = x - mu
    var = jnp.mean(xc * xc, axis=-1, keepdims=True)
    y = xc * lax.rsqrt(var + LN_EPS) * g_ref[...] + b_ref[...]
    of_ref[...] = y
    ob_ref[...] = y.astype(BF16)


def _layer_norm(x, g, b, *, tr):
    rows, d = x.shape
    return pl.pallas_call(
        _ln_kernel,
        grid=(rows // tr,),
        in_specs=[pl.BlockSpec((tr, d), lambda i: (i, 0)),
                  pl.BlockSpec((1, d), lambda i: (0, 0)),
                  pl.BlockSpec((1, d), lambda i: (0, 0))],
        out_specs=[pl.BlockSpec((tr, d), lambda i: (i, 0)),
                   pl.BlockSpec((tr, d), lambda i: (i, 0))],
        out_shape=[jax.ShapeDtypeStruct((rows, d), F32), jax.ShapeDtypeStruct((rows, d), BF16)],
        compiler_params=_params("parallel"),
        name="layer_norm",
    )(x, g.reshape(1, d), b.reshape(1, d))


def _cast_weights_once(w_refs, w_scr):
    @pl.when(pl.program_id(1) == 0)
    def _():
        col = 0
        for w_ref in w_refs:
            w_scr[:, col:col + w_ref.shape[1]] = w_ref[...].astype(BF16)
            col += w_ref.shape[1]


def _mm_kernel(x_ref, w_ref, o_ref, w_scr):
    _cast_weights_once([w_ref], w_scr)
    o_ref[...] = _dot(x_ref[...], w_scr[...]).astype(o_ref.dtype)


def _project(x, w, layer, *, col_off, n_cols, tm, tn, out_dtype):
    m, k = x.shape
    off = col_off // tn
    return pl.pallas_call(
        _mm_kernel,
        grid=(n_cols // tn, m // tm),
        in_specs=[pl.BlockSpec((tm, k), lambda j, i: (i, 0)),
                  pl.BlockSpec((None, k, tn), lambda j, i: (layer, 0, j + off))],
        out_specs=pl.BlockSpec((tm, tn), lambda j, i: (i, j)),
        out_shape=jax.ShapeDtypeStruct((m, n_cols), out_dtype),
        scratch_shapes=[pltpu.VMEM((k, tn), BF16)],
        compiler_params=_params("arbitrary", "arbitrary"),
        name="proj_attn",
    )(x, w)


def _causal_conv3(u, w, scr_ref, halo_ref, first_tile):
    tm = u.shape[0]
    scr_ref[0:CONV_HALO, :] = jnp.where(first_tile, 0.0, halo_ref[...])
    scr_ref[CONV_HALO:CONV_HALO + tm, :] = u
    halo_ref[...] = u[tm - CONV_HALO:tm, :]
    return (w[0:1, :] * scr_ref[CONV_HALO - 2:CONV_HALO - 2 + tm, :]
            + w[1:2, :] * scr_ref[CONV_HALO - 1:CONV_HALO - 1 + tm, :]
            + w[2:3, :] * u)


def _conv_mixer_kernel(x_ref, wb_ref, wc_ref, wh_ref, cw_ref, o_ref, w_scr, scr_ref, halo_ref, *, tiles_per_seq):
    i = pl.program_id(1)
    tn = o_ref.shape[1]
    _cast_weights_once([wb_ref, wc_ref, wh_ref], w_scr)

    @pl.when(i == 0)
    def _():
        halo_ref[...] = jnp.zeros(halo_ref.shape, F32)

    u = _dot(x_ref[...], w_scr[...])
    g = u[:, tn:2 * tn] * u[:, 2 * tn:]
    y = _causal_conv3(g, cw_ref[...], scr_ref, halo_ref, i % tiles_per_seq == 0)
    o_ref[...] = (u[:, :tn] * y).astype(o_ref.dtype)


def _conv_mixer(x, w_in, conv_w, layer, *, width, seq, tm, tn):
    m, k = x.shape
    nt = width // tn
    kern = functools.partial(_conv_mixer_kernel, tiles_per_seq=seq // tm)
    wspec = lambda s: pl.BlockSpec((None, k, tn), lambda j, i: (layer, 0, j + s * nt))
    return pl.pallas_call(
        kern,
        grid=(nt, m // tm),
        in_specs=[pl.BlockSpec((tm, k), lambda j, i: (i, 0)), wspec(0), wspec(1), wspec(2),
                  pl.BlockSpec((None, 3, tn), lambda j, i: (layer, 0, j))],
        out_specs=pl.BlockSpec((tm, tn), lambda j, i: (i, j)),
        out_shape=jax.ShapeDtypeStruct((m, width), BF16),
        scratch_shapes=[pltpu.VMEM((k, 3 * tn), BF16), pltpu.VMEM((CONV_HALO + tm, tn), F32),
                        pltpu.VMEM((CONV_HALO, tn), F32)],
        compiler_params=_params("arbitrary", "arbitrary"),
        name="conv_mixer",
    )(x, w_in, w_in, w_in, conv_w)


def _ffn_up_kernel(x_ref, wg_ref, wu_ref, cg_ref, cu_ref, o_ref, w_scr, scr_ref, halo_ref, *, sub, subs_per_seq):
    i = pl.program_id(1)
    tn = o_ref.shape[1]
    ts = o_ref.shape[0] // sub
    _cast_weights_once([wg_ref, wu_ref], w_scr)

    @pl.when(i == 0)
    def _():
        halo_ref[...] = jnp.zeros(halo_ref.shape, F32)

    cw = jnp.concatenate([cg_ref[...], cu_ref[...]], axis=1)
    for s in range(sub):
        rows = slice(s * ts, (s + 1) * ts)
        u = _dot(x_ref[rows, :], w_scr[...])
        c = _causal_conv3(u, cw, scr_ref.at[s], halo_ref, (i * sub + s) % subs_per_seq == 0)
        gate, up = c[:, :tn], c[:, tn:]
        o_ref[rows, :] = (gate * jax.nn.sigmoid(gate) * up).astype(o_ref.dtype)


def _ffn_up(x, w_up, conv_w, layer, *, d_ff, seq, tm, tn, sub):
    m, k = x.shape
    nt = d_ff // tn
    kern = functools.partial(_ffn_up_kernel, sub=sub, subs_per_seq=seq * sub // tm)
    return pl.pallas_call(
        kern,
        grid=(nt, m // tm),
        in_specs=[pl.BlockSpec((tm, k), lambda j, i: (i, 0)),
                  pl.BlockSpec((None, k, tn), lambda j, i: (layer, 0, j)),
                  pl.BlockSpec((None, k, tn), lambda j, i: (layer, 0, j + nt)),
                  pl.BlockSpec((None, 3, tn), lambda j, i: (layer, 0, j)),
                  pl.BlockSpec((None, 3, tn), lambda j, i: (layer, 0, j + nt))],
        out_specs=pl.BlockSpec((tm, tn), lambda j, i: (i, j)),
        out_shape=jax.ShapeDtypeStruct((m, d_ff), BF16),
        scratch_shapes=[pltpu.VMEM((k, 2 * tn), BF16), pltpu.VMEM((sub, CONV_HALO + tm // sub, 2 * tn), F32),
                        pltpu.VMEM((CONV_HALO, 2 * tn), F32)],
        compiler_params=_params("arbitrary", "arbitrary"),
        name="ffn_up",
    )(x, w_up, w_up, conv_w, conv_w)


def _out_proj_kernel(yc_ref, yd_ref, ys_ref, w_ref, h_ref, o_ref, w_scr, *, alpha):
    _cast_weights_once([w_ref], w_scr)
    r1 = yc_ref.shape[1]
    r2 = r1 + yd_ref.shape[1]
    mix = (_dot(yc_ref[...], w_scr[0:r1, :]) + _dot(yd_ref[...], w_scr[r1:r2, :])
           + _dot(ys_ref[...], w_scr[r2:, :]))
    o_ref[...] = alpha * h_ref[...] + mix


def _out_proj(yc, yd, ys, w, h, layer, *, alpha, tm, tn):
    m, d = h.shape
    k = w.shape[1]
    lhs = lambda a: pl.BlockSpec((tm, a.shape[1]), lambda j, i: (i, 0))
    return pl.pallas_call(
        functools.partial(_out_proj_kernel, alpha=alpha),
        grid=(d // tn, m // tm),
        in_specs=[lhs(yc), lhs(yd), lhs(ys),
                  pl.BlockSpec((None, k, tn), lambda j, i: (layer, 0, j)),
                  pl.BlockSpec((tm, tn), lambda j, i: (i, j))],
        out_specs=pl.BlockSpec((tm, tn), lambda j, i: (i, j)),
        out_shape=jax.ShapeDtypeStruct((m, d), F32),
        scratch_shapes=[pltpu.VMEM((k, tn), BF16)],
        compiler_params=_params("arbitrary", "arbitrary"),
        name="out_proj",
    )(yc, yd, ys, w, h)


def _mm_res_kernel(x_ref, w_ref, h_ref, o_ref, *, alpha):
    o_ref[...] = alpha * h_ref[...] + _dot(x_ref[...], w_ref[...])


def _ffn_down(a, w, h, layer, *, alpha, tm, tn):
    m, d = h.shape
    k = a.shape[1]
    return pl.pallas_call(
        functools.partial(_mm_res_kernel, alpha=alpha),
        grid=(m // tm, d // tn),
        in_specs=[pl.BlockSpec((tm, k), lambda i, j: (i, 0)),
                  pl.BlockSpec((None, k, tn), lambda i, j: (layer, 0, j)),
                  pl.BlockSpec((tm, tn), lambda i, j: (i, j))],
        out_specs=pl.BlockSpec((tm, tn), lambda i, j: (i, j)),
        out_shape=jax.ShapeDtypeStruct((m, d), F32),
        compiler_params=_params("parallel", "arbitrary"),
        name="ffn_down",
    )(a, w, h)


def _pad_rows(a, rows):
    if a.shape[0] == rows:
        return a
    return jnp.concatenate([a, jnp.zeros((rows - a.shape[0], a.shape[1]), a.dtype)], axis=0)


def _head_cols(h):
    return slice(h * HEAD_DIM, (h + 1) * HEAD_DIM)


def _num_key_blocks(seq):
    return -(-seq // ATT_BLOCK)


def _stage_values_transposed(v_ref, vt_ref, seq, nh):
    for h in range(nh):
        for j in range(_num_key_blocks(seq)):
            r0 = j * ATT_BLOCK
            v = _pad_rows(v_ref[r0:min(r0 + ATT_BLOCK, seq), _head_cols(h)], ATT_BLOCK)
            vt_ref[h, j, 0:HEAD_DIM, :] = v.astype(F32).T.astype(BF16)


def _key_block(k_ref, j, t, nh):
    r0 = j * ATT_BLOCK
    if not isinstance(j, int):
        r0 = pl.multiple_of(r0, ATT_BLOCK)
    return [_pad_rows(k_ref[pl.ds(r0, t), _head_cols(h)], ATT_BLOCK) for h in range(nh)]


def _causal_sweep(seq, q_block):
    n_full, tail = seq // ATT_BLOCK, seq % ATT_BLOCK

    def full_block(i, _):
        q_block(i, ATT_BLOCK, ATT_BLOCK)
        return 0

    lax.fori_loop(0, n_full, full_block, 0)
    if tail:
        q_block(n_full, tail, ATT_TAIL_LANES)


def _key_query_iota(tp, maps):
    r = lax.broadcasted_iota(jnp.int32, (ATT_BLOCK, maps * tp), 0)
    c = lax.broadcasted_iota(jnp.int32, (ATT_BLOCK, maps * tp), 1)
    return r, (jnp.where(c >= tp, c - tp, c) if maps == 2 else c)


def _diff_attn_kernel(lam_ref, q_ref, k_ref, v_ref, g_ref, o_ref, vt_ref, *, seq, gain, nh):
    blk = ATT_BLOCK
    heads = range(nh)
    lam = lam_ref[0]
    g = g_ref[...]
    lo_lane = lax.broadcasted_iota(jnp.int32, (1, HEAD_DIM), 1) < DIFF_QK_DIM
    _stage_values_transposed(v_ref, vt_ref, seq, nh)
    for h in heads:
        for j in range(_num_key_blocks(seq)):
            vt_ref[h, j, HEAD_DIM:, :] = jnp.ones((ONES_ROWS, blk), BF16)

    def stacked_q(h, q0, t, tp):
        q = (q_ref[pl.ds(q0, t), _head_cols(h)].astype(F32) * (DIFF_QK_DIM ** -0.5)).astype(BF16)
        q = _pad_rows(q, tp)
        zero = jnp.zeros_like(q)
        return jnp.concatenate([jnp.where(lo_lane, q, zero), jnp.where(lo_lane, zero, q)], axis=0)

    def block_step(qs, ks, j, mask, ms, accs):
        ss = [_dot_nt(ks[h], qs[h]) for h in heads]
        if mask is not None:
            ss = [jnp.where(mask, s, -jnp.inf) for s in ss]
        m_new = [jnp.maximum(ms[h], jnp.max(ss[h], axis=0, keepdims=True)) for h in heads]
        ps = [jnp.exp(ss[h] - m_new[h]).astype(BF16) for h in heads]
        pvs = [_dot(vt_ref[h, j], ps[h]) for h in heads]
        accs = [jnp.exp(ms[h] - m_new[h]) * accs[h] + pvs[h] for h in heads]
        return m_new, accs

    def q_block(i, t, tp):
        q0 = i * blk if isinstance(i, int) else pl.multiple_of(i * blk, blk)
        qs = [stacked_q(h, q0, t, tp) for h in heads]
        init = ((jnp.full((1, 2 * tp), -jnp.inf, F32),) * nh
                + (jnp.zeros((HEAD_DIM + ONES_ROWS, 2 * tp), F32),) * nh)

        def kv_step(j, carry):
            ms, accs = block_step(qs, _key_block(k_ref, j, blk, nh), j, None, carry[:nh], carry[nh:])
            return tuple(ms) + tuple(accs)

        carry = lax.fori_loop(0, i, kv_step, init)
        r, c = _key_query_iota(tp, 2)
        _, accs = block_step(qs, _key_block(k_ref, i, t, nh), i, r <= c, carry[:nh], carry[nh:])
        for h in heads:
            o = accs[h][:HEAD_DIM] * (1.0 / accs[h][HEAD_DIM:HEAD_DIM + 1])
            o = o[:, :tp] - lam * o[:, tp:]
            o = o * lax.rsqrt(jnp.mean(o * o, axis=0, keepdims=True) + LN_EPS)
            o = (o * g * gain).T
            o_ref[pl.ds(q0, t), _head_cols(h)] = o[:t].astype(o_ref.dtype)

    _causal_sweep(seq, q_block)


def _diff_attention(proj, lam, gain_g, *, heads, col_off, lam_init, nh):
    b, seq, _ = proj.shape
    w = nh * HEAD_DIM
    c0 = col_off // w
    spec = lambda s: pl.BlockSpec((None, seq, w), lambda i, h: (i, 0, c0 + s * (heads // nh) + h))
    return pl.pallas_call(
        functools.partial(_diff_attn_kernel, seq=seq, gain=1.0 - lam_init, nh=nh),
        grid=(b, heads // nh),
        in_specs=[pl.BlockSpec(memory_space=pltpu.SMEM), spec(0), spec(1), spec(2),
                  pl.BlockSpec((HEAD_DIM, 1), lambda i, h: (0, 0))],
        out_specs=pl.BlockSpec((None, seq, w), lambda i, h: (i, 0, h)),
        out_shape=jax.ShapeDtypeStruct((b, seq, heads * HEAD_DIM), BF16),
        scratch_shapes=[pltpu.VMEM((nh, _num_key_blocks(seq), HEAD_DIM + ONES_ROWS, ATT_BLOCK), BF16)],
        compiler_params=_params("parallel", "parallel"),
        name="diff_attn",
    )(lam, proj, proj, proj, gain_g.reshape(HEAD_DIM, 1))


def _sb_attn_kernel(q_ref, k_ref, v_ref, a_ref, o_ref, vt_ref, *, seq, nh):
    blk = ATT_BLOCK
    heads = range(nh)
    scale = HEAD_DIM ** -0.5
    _stage_values_transposed(v_ref, vt_ref, seq, nh)

    def block_step(qs, ks, j, mask, runs, accs):
        zs = [_dot_nt(ks[h], qs[h]) * scale for h in heads]
        sps = [jnp.maximum(z, 0.0) + jnp.log(1.0 + jnp.exp(-jnp.abs(z))) for z in zs]
        log_keeps = [-sp if mask is None else jnp.where(mask, -sp, 0.0) for sp in sps]
        his = [lk.astype(BF16) for lk in log_keeps]
        los = [(lk - hi.astype(F32)).astype(BF16) for lk, hi in zip(log_keeps, his)]
        incls = [_dot(a_ref[...], jnp.concatenate([his[h], los[h]], axis=0)) for h in heads]
        ws = [jnp.exp(zs[h] + incls[h] + runs[h]) for h in heads]
        if mask is not None:
            ws = [jnp.where(mask, w, 0.0) for w in ws]
        pvs = [_dot(vt_ref[h, j], ws[h].astype(BF16)) for h in heads]
        runs = [runs[h] + incls[h][0:1, :] for h in heads]
        accs = [pvs[h] if accs is None else accs[h] + pvs[h] for h in heads]
        return runs, accs

    def q_block(i, t, tp):
        q0 = i * blk if isinstance(i, int) else pl.multiple_of(i * blk, blk)
        qs = [_pad_rows(q_ref[pl.ds(q0, t), _head_cols(h)], tp) for h in heads]
        r, c = _key_query_iota(tp, 1)
        runs, accs = block_step(qs, _key_block(k_ref, i, t, nh), i, r < c, [jnp.zeros((1, tp), F32)] * nh, None)

        def kv_step(jj, carry):
            j = i - 1 - jj
            runs, accs = block_step(qs, _key_block(k_ref, j, blk, nh), j, None, carry[:nh], carry[nh:])
            return tuple(runs) + tuple(accs)

        carry = lax.fori_loop(0, i, kv_step, tuple(runs) + tuple(accs))
        for h in heads:
            o_ref[pl.ds(q0, t), _head_cols(h)] = carry[nh + h].T[:t].astype(o_ref.dtype)

    _causal_sweep(seq, q_block)


def _sb_attention(proj, *, heads, col_off, nh):
    b, seq, _ = proj.shape
    w = nh * HEAD_DIM
    c0 = col_off // w
    blk = ATT_BLOCK
    tri = (lax.broadcasted_iota(jnp.int32, (blk, blk), 1) >= lax.broadcasted_iota(jnp.int32, (blk, blk), 0))
    a2 = jnp.concatenate([tri, tri], axis=1).astype(BF16)
    spec = lambda s: pl.BlockSpec((None, seq, w), lambda i, h: (i, 0, c0 + s * (heads // nh) + h))
    return pl.pallas_call(
        functools.partial(_sb_attn_kernel, seq=seq, nh=nh),
        grid=(b, heads // nh),
        in_specs=[spec(0), spec(1), spec(2), pl.BlockSpec((blk, 2 * blk), lambda i, h: (0, 0))],
        out_specs=pl.BlockSpec((None, seq, w), lambda i, h: (i, 0, h)),
        out_shape=jax.ShapeDtypeStruct((b, seq, heads * HEAD_DIM), BF16),
        scratch_shapes=[pltpu.VMEM((nh, _num_key_blocks(seq), HEAD_DIM, ATT_BLOCK), BF16)],
        compiler_params=_params("parallel", "parallel"),
        name="sb_attn",
    )(proj, proj, proj, a2)


def kernel(x, meta_tokens, emb_ln_g, emb_ln_b, w_in, short_conv_w, lambda_q1, lambda_k1, lambda_q2, lambda_k2,
           diff_norm_g, w_out, ln1_g, ln1_b, w_up, ffn_conv_w, w_down, ln2_g, ln2_b):
    bsz, _, d = x.shape
    depth = w_in.shape[0]
    conv_w = short_conv_w.shape[-1]
    d_ff = w_down.shape[1]
    diff_w = (w_in.shape[-1] - 3 * conv_w) // 6
    heads = diff_w // HEAD_DIM
    alpha = (2 * depth) ** 0.25

    meta = jnp.broadcast_to(meta_tokens[None].astype(x.dtype), (bsz, N_META, d))
    tokens = jnp.concatenate([meta, x], axis=1)
    seq = tokens.shape[1]
    rows = bsz * seq
    tm_seq = seq // 3
    tm_big = 2 * tm_seq
    tr = _pick_tile(rows, (192, 176, 96, 48, 16))
    tn_wide = _pick_tile(math.gcd(6 * diff_w, d), (512, 256, 128))
    tn_conv = _pick_tile(math.gcd(conv_w, d_ff), (256, 128))
    nh = _pick_tile(heads, (ATT_HEADS_PER_STEP, 1))

    w_down_b = w_down.astype(BF16)

    h, hb = _layer_norm(tokens.reshape(rows, d), emb_ln_g, emb_ln_b, tr=tr)
    for l in range(depth):
        lam_init = _lambda_init(l)
        lam = (jnp.exp(jnp.sum(lambda_q1[l] * lambda_k1[l])) - jnp.exp(jnp.sum(lambda_q2[l] * lambda_k2[l]))
               + lam_init).reshape(1).astype(F32)
        y_conv = _conv_mixer(hb, w_in, short_conv_w, l, width=conv_w, seq=seq, tm=tm_seq, tn=tn_conv)
        proj = _project(hb, w_in, l, col_off=3 * conv_w, n_cols=6 * diff_w, tm=tm_big, tn=tn_wide, out_dtype=BF16)
        proj = proj.reshape(bsz, seq, 6 * diff_w)
        y_diff = _diff_attention(proj, lam, diff_norm_g[l], heads=heads, col_off=0, lam_init=lam_init, nh=nh)
        y_sb = _sb_attention(proj, heads=heads, col_off=3 * diff_w, nh=nh)
        x1 = _out_proj(y_conv, y_diff.reshape(rows, diff_w), y_sb.reshape(rows, diff_w),
                       w_out, h, l, alpha=alpha, tm=tm_seq, tn=tn_wide)
        h, hb = _layer_norm(x1, ln1_g[l], ln1_b[l], tr=tr)
        act = _ffn_up(hb, w_up, ffn_conv_w, l, d_ff=d_ff, seq=seq, tm=tm_big, tn=tn_conv, sub=2)
        x2 = _ffn_down(act, w_down_b, h, l, alpha=alpha, tm=tm_seq, tn=tn_conv)
        h, hb = _layer_norm(x2, ln2_g[l], ln2_b[l], tr=tr)
    return h.reshape(bsz, seq, d)[:, N_META:]
```

```python
import functools
import math

import jax
import jax.numpy as jnp
from jax import lax
from jax.experimental import pallas as pl
from jax.experimental.pallas import tpu as pltpu

N_META = 16
HEAD_DIM = 128
DIFF_QK_DIM = HEAD_DIM // 2
LN_EPS = 1e-5
ATT_BLOCK = 256
ATT_HEADS_PER_STEP = 6
ATT_TAIL_LANES = 128
ONES_ROWS = 16
CONV_HALO = 8
VMEM_LIMIT = 56 * 1024 * 1024

F32 = jnp.float32
BF16 = jnp.bfloat16


def _lambda_init(layer):
    return 0.8 - 0.6 * math.exp(-0.3 * layer)


def _pick_tile(n, candidates):
    return next(c for c in candidates if n % c == 0)


def _params(*sem):
    return pltpu.CompilerParams(dimension_semantics=sem, vmem_limit_bytes=VMEM_LIMIT)


def _dot(a, b):
    return jnp.dot(a, b, preferred_element_type=F32)


def _dot_nt(a, b):
    return lax.dot_general(a, b, (((1,), (1,)), ((), ())), preferred_element_type=F32)


def _ln_kernel(x_ref, g_ref, b_ref, of_ref, ob_ref):
    x = x_ref[...]
    mu = jnp.mean(x, axis=-1, keepdims=True)
    xc = x - mu
    var = jnp.mean(xc * xc, axis=-1, keepdims=True)
    y = xc * lax.rsqrt(var + LN_EPS) * g_ref[...] + b_ref[...]
    of_ref[...] = y
    ob_ref[...] = y.astype(BF16)


def _layer_norm(x, g, b, *, tr):
    rows, d = x.shape
    return pl.pallas_call(
        _ln_kernel,
        grid=(rows // tr,),
        in_specs=[pl.BlockSpec((tr, d), lambda i: (i, 0)),
                  pl.BlockSpec((1, d), lambda i: (0, 0)),
                  pl.BlockSpec((1, d), lambda i: (0, 0))],
        out_specs=[pl.BlockSpec((tr, d), lambda i: (i, 0)),
                   pl.BlockSpec((tr, d), lambda i: (i, 0))],
        out_shape=[jax.ShapeDtypeStruct((rows, d), F32), jax.ShapeDtypeStruct((rows, d), BF16)],
        compiler_params=_params("parallel"),
        name="layer_norm",
    )(x, g.reshape(1, d), b.reshape(1, d))


def _cast_weights_once(w_refs, w_scr):
    @pl.when(pl.program_id(1) == 0)
    def _():
        col = 0
        for w_ref in w_refs:
            w_scr[:, col:col + w_ref.shape[1]] = w_ref[...].astype(BF16)
            col += w_ref.shape[1]


def _mm_kernel(x_ref, w_ref, o_ref, w_scr):
    _cast_weights_once([w_ref], w_scr)
    o_ref[...] = _dot(x_ref[...], w_scr[...]).astype(o_ref.dtype)


def _project(x, w, layer, *, col_off, n_cols, tm, tn, out_dtype):
    m, k = x.shape
    off = col_off // tn
    return pl.pallas_call(
        _mm_kernel,
        grid=(n_cols // tn, m // tm),
        in_specs=[pl.BlockSpec((tm, k), lambda j, i: (i, 0)),
                  pl.BlockSpec((None, k, tn), lambda j, i: (layer, 0, j + off))],
        out_specs=pl.BlockSpec((tm, tn), lambda j, i: (i, j)),
        out_shape=jax.ShapeDtypeStruct((m, n_cols), out_dtype),
        scratch_shapes=[pltpu.VMEM((k, tn), BF16)],
        compiler_params=_params("arbitrary", "arbitrary"),
        name="proj_attn",
    )(x, w)


def _causal_conv3(u, w, scr_ref, halo_ref, first_tile):
    tm = u.shape[0]
    scr_ref[0:CONV_HALO, :] = jnp.where(first_tile, 0.0, halo_ref[...])
    scr_ref[CONV_HALO:CONV_HALO + tm, :] = u
    halo_ref[...] = u[tm - CONV_HALO:tm, :]
    return (w[0:1, :] * scr_ref[CONV_HALO - 2:CONV_HALO - 2 + tm, :]
            + w[1:2, :] * scr_ref[CONV_HALO - 1:CONV_HALO - 1 + tm, :]
            + w[2:3, :] * u)


def _conv_mixer_kernel(x_ref, wb_ref, wc_ref, wh_ref, cw_ref, o_ref, w_scr, scr_ref, halo_ref, *, tiles_per_seq):
    i = pl.program_id(1)
    tn = o_ref.shape[1]
    _cast_weights_once([wb_ref, wc_ref, wh_ref], w_scr)

    @pl.when(i == 0)
    def _():
        halo_ref[...] = jnp.zeros(halo_ref.shape, F32)

    u = _dot(x_ref[...], w_scr[...])
    g = u[:, tn:2 * tn] * u[:, 2 * tn:]
    y = _causal_conv3(g, cw_ref[...], scr_ref, halo_ref, i % tiles_per_seq == 0)
    o_ref[...] = (u[:, :tn] * y).astype(o_ref.dtype)


def _conv_mixer(x, w_in, conv_w, layer, *, width, seq, tm, tn):
    m, k = x.shape
    nt = width // tn
    kern = functools.partial(_conv_mixer_kernel, tiles_per_seq=seq // tm)
    wspec = lambda s: pl.BlockSpec((None, k, tn), lambda j, i: (layer, 0, j + s * nt))
    return pl.pallas_call(
        kern,
        grid=(nt, m // tm),
        in_specs=[pl.BlockSpec((tm, k), lambda j, i: (i, 0)), wspec(0), wspec(1), wspec(2),
                  pl.BlockSpec((None, 3, tn), lambda j, i: (layer, 0, j))],
        out_specs=pl.BlockSpec((tm, tn), lambda j, i: (i, j)),
        out_shape=jax.ShapeDtypeStruct((m, width), BF16),
        scratch_shapes=[pltpu.VMEM((k, 3 * tn), BF16), pltpu.VMEM((CONV_HALO + tm, tn), F32),
                        pltpu.VMEM((CONV_HALO, tn), F32)],
        compiler_params=_params("arbitrary", "arbitrary"),
        name="conv_mixer",
    )(x, w_in, w_in, w_in, conv_w)


def _ffn_up_kernel(x_ref, wg_ref, wu_ref, cg_ref, cu_ref, o_ref, w_scr, scr_ref, halo_ref, *, sub, subs_per_seq):
    i = pl.program_id(1)
    tn = o_ref.shape[1]
    ts = o_ref.shape[0] // sub
    _cast_weights_once([wg_ref, wu_ref], w_scr)

    @pl.when(i == 0)
    def _():
        halo_ref[...] = jnp.zeros(halo_ref.shape, F32)

    cw = jnp.concatenate([cg_ref[...], cu_ref[...]], axis=1)
    for s in range(sub):
        rows = slice(s * ts, (s + 1) * ts)
        u = _dot(x_ref[rows, :], w_scr[...])
        c = _causal_conv3(u, cw, scr_ref.at[s], halo_ref, (i * sub + s) % subs_per_seq == 0)
        gate, up = c[:, :tn], c[:, tn:]
        o_ref[rows, :] = (gate * jax.nn.sigmoid(gate) * up).astype(o_ref.dtype)


def _ffn_up(x, w_up, conv_w, layer, *, d_ff, seq, tm, tn, sub):
    m, k = x.shape
    nt = d_ff // tn
    kern = functools.partial(_ffn_up_kernel, sub=sub, subs_per_seq=seq * sub // tm)
    return pl.pallas_call(
        kern,
        grid=(nt, m // tm),
        in_specs=[pl.BlockSpec((tm, k), lambda j, i: (i, 0)),
                  pl.BlockSpec((None, k, tn), lambda j, i: (layer, 0, j)),
                  pl.BlockSpec((None, k, tn), lambda j, i: (layer, 0, j + nt)),
                  pl.BlockSpec((None, 3, tn), lambda j, i: (layer, 0, j)),
                  pl.BlockSpec((None, 3, tn), lambda j, i: (layer, 0, j + nt))],
        out_specs=pl.BlockSpec((tm, tn), lambda j, i: (i, j)),
        out_shape=jax.ShapeDtypeStruct((m, d_ff), BF16),
        scratch_shapes=[pltpu.VMEM((k, 2 * tn), BF16), pltpu.VMEM((sub, CONV_HALO + tm // sub, 2 * tn), F32),
                        pltpu.VMEM((CONV_HALO, 2 * tn), F32)],
        compiler_params=_params("arbitrary", "arbitrary"),
        name="ffn_up",
    )(x, w_up, w_up, conv_w, conv_w)


def _out_proj_kernel(yc_ref, yd_ref, ys_ref, wc_ref, wd_ref, ws_ref, h_ref, o_ref, *, alpha):
    mix = _dot(yc_ref[...], wc_ref[...]) + _dot(yd_ref[...], wd_ref[...]) + _dot(ys_ref[...], ws_ref[...])
    o_ref[...] = alpha * h_ref[...] + mix


def _out_proj(yc, yd, ys, wc, wd, ws, h, layer, *, alpha, tm, tn):
    m, d = h.shape
    lhs = lambda a: pl.BlockSpec((tm, a.shape[1]), lambda i, j: (i, 0))
    rhs = lambda w: pl.BlockSpec((None, w.shape[1], tn), lambda i, j: (layer, 0, j))
    return pl.pallas_call(
        functools.partial(_out_proj_kernel, alpha=alpha),
        grid=(m // tm, d // tn),
        in_specs=[lhs(yc), lhs(yd), lhs(ys), rhs(wc), rhs(wd), rhs(ws),
                  pl.BlockSpec((tm, tn), lambda i, j: (i, j))],
        out_specs=pl.BlockSpec((tm, tn), lambda i, j: (i, j)),
        out_shape=jax.ShapeDtypeStruct((m, d), F32),
        compiler_params=_params("parallel", "arbitrary"),
        name="out_proj",
    )(yc, yd, ys, wc, wd, ws, h)


def _mm_res_kernel(x_ref, w_ref, h_ref, o_ref, *, alpha):
    o_ref[...] = alpha * h_ref[...] + _dot(x_ref[...], w_ref[...])


def _ffn_down(a, w, h, layer, *, alpha, tm, tn):
    m, d = h.shape
    k = a.shape[1]
    return pl.pallas_call(
        functools.partial(_mm_res_kernel, alpha=alpha),
        grid=(m // tm, d // tn),
        in_specs=[pl.BlockSpec((tm, k), lambda i, j: (i, 0)),
                  pl.BlockSpec((None, k, tn), lambda i, j: (layer, 0, j)),
                  pl.BlockSpec((tm, tn), lambda i, j: (i, j))],
        out_specs=pl.BlockSpec((tm, tn), lambda i, j: (i, j)),
        out_shape=jax.ShapeDtypeStruct((m, d), F32),
        compiler_params=_params("parallel", "arbitrary"),
        name="ffn_down",
    )(a, w, h)


def _pad_rows(a, rows):
    if a.shape[0] == rows:
        return a
    return jnp.concatenate([a, jnp.zeros((rows - a.shape[0], a.shape[1]), a.dtype)], axis=0)


def _head_cols(h):
    return slice(h * HEAD_DIM, (h + 1) * HEAD_DIM)


def _num_key_blocks(seq):
    return -(-seq // ATT_BLOCK)


def _stage_values_transposed(v_ref, vt_ref, seq, nh):
    for h in range(nh):
        for j in range(_num_key_blocks(seq)):
            r0 = j * ATT_BLOCK
            v = _pad_rows(v_ref[r0:min(r0 + ATT_BLOCK, seq), _head_cols(h)], ATT_BLOCK)
            vt_ref[h, j, 0:HEAD_DIM, :] = v.astype(F32).T.astype(BF16)


def _row_block(ref, j, t, nh):
    r0 = j * ATT_BLOCK
    if not isinstance(j, int):
        r0 = pl.multiple_of(r0, ATT_BLOCK)
    return [_pad_rows(ref[pl.ds(r0, t), _head_cols(h)], ATT_BLOCK) for h in range(nh)]


def _causal_sweep(seq, q_block):
    n_full, tail = seq // ATT_BLOCK, seq % ATT_BLOCK

    def full_block(i, _):
        q_block(i, ATT_BLOCK, ATT_BLOCK)
        return 0

    lax.fori_loop(0, n_full, full_block, 0)
    if tail:
        q_block(n_full, tail, ATT_TAIL_LANES)


def _key_query_iota(tp, maps):
    r = lax.broadcasted_iota(jnp.int32, (ATT_BLOCK, maps * tp), 0)
    c = lax.broadcasted_iota(jnp.int32, (ATT_BLOCK, maps * tp), 1)
    return r, (jnp.where(c >= tp, c - tp, c) if maps == 2 else c)


def _diff_attn_kernel(lam_ref, q_ref, k_ref, v_ref, g_ref, o_ref, vt_ref, *, seq, gain, nh):
    blk = ATT_BLOCK
    heads = range(nh)
    lam = lam_ref[0]
    g = g_ref[...]
    lo_lane = lax.broadcasted_iota(jnp.int32, (1, HEAD_DIM), 1) < DIFF_QK_DIM
    _stage_values_transposed(v_ref, vt_ref, seq, nh)
    for h in heads:
        for j in range(_num_key_blocks(seq)):
            vt_ref[h, j, HEAD_DIM:, :] = jnp.ones((ONES_ROWS, blk), BF16)

    def stacked_q(h, q0, t, tp):
        q = (q_ref[pl.ds(q0, t), _head_cols(h)].astype(F32) * (DIFF_QK_DIM ** -0.5)).astype(BF16)
        q = _pad_rows(q, tp)
        zero = jnp.zeros_like(q)
        return jnp.concatenate([jnp.where(lo_lane, q, zero), jnp.where(lo_lane, zero, q)], axis=0)

    def block_step(qs, ks, j, mask, ms, accs):
        ss = [_dot_nt(ks[h], qs[h]) for h in heads]
        if mask is not None:
            ss = [jnp.where(mask, s, -jnp.inf) for s in ss]
        m_new = [jnp.maximum(ms[h], jnp.max(ss[h], axis=0, keepdims=True)) for h in heads]
        ps = [jnp.exp(ss[h] - m_new[h]).astype(BF16) for h in heads]
        pvs = [_dot(vt_ref[h, j], ps[h]) for h in heads]
        accs = [jnp.exp(ms[h] - m_new[h]) * accs[h] + pvs[h] for h in heads]
        return m_new, accs

    def q_block(i, t, tp):
        q0 = i * blk if isinstance(i, int) else pl.multiple_of(i * blk, blk)
        qs = [stacked_q(h, q0, t, tp) for h in heads]
        init = ((jnp.full((1, 2 * tp), -jnp.inf, F32),) * nh
                + (jnp.zeros((HEAD_DIM + ONES_ROWS, 2 * tp), F32),) * nh)

        def kv_step(j, carry):
            ms, accs = block_step(qs, _row_block(k_ref, j, blk, nh), j, None, carry[:nh], carry[nh:])
            return tuple(ms) + tuple(accs)

        carry = lax.fori_loop(0, i, kv_step, init)
        r, c = _key_query_iota(tp, 2)
        _, accs = block_step(qs, _row_block(k_ref, i, t, nh), i, r <= c, carry[:nh], carry[nh:])
        for h in heads:
            o = accs[h][:HEAD_DIM] * (1.0 / accs[h][HEAD_DIM:HEAD_DIM + 1])
            o = o[:, :tp] - lam * o[:, tp:]
            o = o * lax.rsqrt(jnp.mean(o * o, axis=0, keepdims=True) + LN_EPS)
            o = (o * g * gain).T
            o_ref[pl.ds(q0, t), _head_cols(h)] = o[:t].astype(o_ref.dtype)

    _causal_sweep(seq, q_block)


def _diff_attention(proj, lam, gain_g, *, heads, col_off, lam_init, nh):
    b, seq, _ = proj.shape
    w = nh * HEAD_DIM
    c0 = col_off // w
    spec = lambda s: pl.BlockSpec((None, seq, w), lambda i, h: (i, 0, c0 + s * (heads // nh) + h))
    return pl.pallas_call(
        functools.partial(_diff_attn_kernel, seq=seq, gain=1.0 - lam_init, nh=nh),
        grid=(b, heads // nh),
        in_specs=[pl.BlockSpec(memory_space=pltpu.SMEM), spec(0), spec(1), spec(2),
                  pl.BlockSpec((HEAD_DIM, 1), lambda i, h: (0, 0))],
        out_specs=pl.BlockSpec((None, seq, w), lambda i, h: (i, 0, h)),
        out_shape=jax.ShapeDtypeStruct((b, seq, heads * HEAD_DIM), BF16),
        scratch_shapes=[pltpu.VMEM((nh, _num_key_blocks(seq), HEAD_DIM + ONES_ROWS, ATT_BLOCK), BF16)],
        compiler_params=_params("parallel", "parallel"),
        name="diff_attn",
    )(lam, proj, proj, proj, gain_g.reshape(HEAD_DIM, 1))


def _sb_attn_kernel(q_ref, k_ref, v_ref, u_ref, o_ref, *, seq, nh):
    blk = ATT_BLOCK
    heads = range(nh)
    scale = HEAD_DIM ** -0.5

    def block_step(qs, ks, vs, mask, runs, accs):
        zs = [_dot_nt(qs[h], ks[h]) * scale for h in heads]
        sps = [jnp.maximum(z, 0.0) + jnp.log(1.0 + jnp.exp(-jnp.abs(z))) for z in zs]
        log_keeps = [-sp if mask is None else jnp.where(mask, -sp, 0.0) for sp in sps]
        his = [lk.astype(BF16) for lk in log_keeps]
        los = [(lk - hi.astype(F32)).astype(BF16) for lk, hi in zip(log_keeps, his)]
        incls = [_dot(jnp.concatenate([his[h], los[h]], axis=1), u_ref[...]) for h in heads]
        ws = [jnp.exp(zs[h] + incls[h] + runs[h]) for h in heads]
        if mask is not None:
            ws = [jnp.where(mask, w, 0.0) for w in ws]
        pvs = [_dot(ws[h].astype(BF16), vs[h]) for h in heads]
        runs = [runs[h] + jnp.sum(log_keeps[h], axis=-1, keepdims=True) for h in heads]
        accs = [pvs[h] if accs is None else accs[h] + pvs[h] for h in heads]
        return runs, accs

    def q_block(i, t, tp):
        del tp
        q0 = i * blk if isinstance(i, int) else pl.multiple_of(i * blk, blk)
        qs = [q_ref[pl.ds(q0, t), _head_cols(h)] for h in heads]
        r = lax.broadcasted_iota(jnp.int32, (t, blk), 0)
        c = lax.broadcasted_iota(jnp.int32, (t, blk), 1)
        runs, accs = block_step(qs, _row_block(k_ref, i, t, nh), _row_block(v_ref, i, t, nh), c < r,
                                [jnp.zeros((t, 1), F32)] * nh, None)

        def kv_step(jj, carry):
            j = i - 1 - jj
            runs, accs = block_step(qs, _row_block(k_ref, j, blk, nh), _row_block(v_ref, j, blk, nh), None,
                                    carry[:nh], carry[nh:])
            return tuple(runs) + tuple(accs)

        carry = lax.fori_loop(0, i, kv_step, tuple(runs) + tuple(accs))
        for h in heads:
            o_ref[pl.ds(q0, t), _head_cols(h)] = carry[nh + h].astype(o_ref.dtype)

    _causal_sweep(seq, q_block)


def _sb_attention(proj, *, heads, col_off, nh):
    b, seq, _ = proj.shape
    w = nh * HEAD_DIM
    c0 = col_off // w
    blk = ATT_BLOCK
    tri = (lax.broadcasted_iota(jnp.int32, (blk, blk), 0) >= lax.broadcasted_iota(jnp.int32, (blk, blk), 1))
    u2 = jnp.concatenate([tri, tri], axis=0).astype(BF16)
    spec = lambda s: pl.BlockSpec((None, seq, w), lambda i, h: (i, 0, c0 + s * (heads // nh) + h))
    return pl.pallas_call(
        functools.partial(_sb_attn_kernel, seq=seq, nh=nh),
        grid=(b, heads // nh),
        in_specs=[spec(0), spec(1), spec(2), pl.BlockSpec((2 * blk, blk), lambda i, h: (0, 0))],
        out_specs=pl.BlockSpec((None, seq, w), lambda i, h: (i, 0, h)),
        out_shape=jax.ShapeDtypeStruct((b, seq, heads * HEAD_DIM), BF16),
        compiler_params=_params("parallel", "parallel"),
        name="sb_attn",
    )(proj, proj, proj, u2)


def kernel(x, meta_tokens, emb_ln_g, emb_ln_b, w_in, short_conv_w, lambda_q1, lambda_k1, lambda_q2, lambda_k2,
           diff_norm_g, w_out, ln1_g, ln1_b, w_up, ffn_conv_w, w_down, ln2_g, ln2_b):
    bsz, _, d = x.shape
    depth = w_in.shape[0]
    conv_w = short_conv_w.shape[-1]
    d_ff = w_down.shape[1]
    diff_w = (w_in.shape[-1] - 3 * conv_w) // 6
    heads = diff_w // HEAD_DIM
    alpha = (2 * depth) ** 0.25

    meta = jnp.broadcast_to(meta_tokens[None].astype(x.dtype), (bsz, N_META, d))
    tokens = jnp.concatenate([meta, x], axis=1)
    seq = tokens.shape[1]
    rows = bsz * seq
    tm_seq = seq // 3
    tm_big = 2 * tm_seq
    tr = _pick_tile(rows, (192, 176, 96, 48, 16))
    tn_wide = _pick_tile(math.gcd(6 * diff_w, d), (512, 256, 128))
    tn_conv = _pick_tile(math.gcd(conv_w, d_ff), (256, 128))
    nh = _pick_tile(heads, (ATT_HEADS_PER_STEP, 1))

    w_down_b = w_down.astype(BF16)
    w_out_c = w_out[:, :conv_w].astype(BF16)
    w_out_d = w_out[:, conv_w:conv_w + diff_w].astype(BF16)
    w_out_s = w_out[:, conv_w + diff_w:].astype(BF16)

    h, hb = _layer_norm(tokens.reshape(rows, d), emb_ln_g, emb_ln_b, tr=tr)
    for l in range(depth):
        lam_init = _lambda_init(l)
        lam = (jnp.exp(jnp.sum(lambda_q1[l] * lambda_k1[l])) - jnp.exp(jnp.sum(lambda_q2[l] * lambda_k2[l]))
               + lam_init).reshape(1).astype(F32)
        y_conv = _conv_mixer(hb, w_in, short_conv_w, l, width=conv_w, seq=seq, tm=tm_seq, tn=tn_conv)
        proj = _project(hb, w_in, l, col_off=3 * conv_w, n_cols=6 * diff_w, tm=tm_big, tn=tn_wide, out_dtype=BF16)
        proj = proj.reshape(bsz, seq, 6 * diff_w)
        y_diff = _diff_attention(proj, lam, diff_norm_g[l], heads=heads, col_off=0, lam_init=lam_init, nh=nh)
        y_sb = _sb_attention(proj, heads=heads, col_off=3 * diff_w, nh=nh)
        x1 = _out_proj(y_conv, y_diff.reshape(rows, diff_w), y_sb.reshape(rows, diff_w),
                       w_out_c, w_out_d, w_out_s, h, l, alpha=alpha, tm=tm_big, tn=tn_wide)
        h, hb = _layer_norm(x1, ln1_g[l], ln1_b[l], tr=tr)
        act = _ffn_up(hb, w_up, ffn_conv_w, l, d_ff=d_ff, seq=seq, tm=tm_big, tn=tn_conv, sub=2)
        x2 = _ffn_down(act, w_down_b, h, l, alpha=alpha, tm=tm_seq, tn=tn_conv)
        h, hb = _layer_norm(x2, ln2_g[l], ln2_b[l], tr=tr)
    return h.reshape(bsz, seq, d)[:, N_META:]
```

```python
import functools
import math

import jax
import jax.numpy as jnp
from jax import lax
from jax.experimental import pallas as pl
from jax.experimental.pallas import tpu as pltpu

N_META = 16
HEAD_DIM = 128
DIFF_QK_DIM = HEAD_DIM // 2
LN_EPS = 1e-5
ATT_BLOCK = 256
ATT_HEADS_PER_STEP = 6
ATT_TAIL_LANES = 128
ONES_ROWS = 16
CONV_HALO = 8
VMEM_LIMIT = 60 * 1024 * 1024

F32 = jnp.float32
BF16 = jnp.bfloat16


def _lambda_init(layer):
    return 0.8 - 0.6 * math.exp(-0.3 * layer)


def _pick_tile(n, candidates):
    return next(c for c in candidates if n % c == 0)


def _params(*sem):
    return pltpu.CompilerParams(dimension_semantics=sem, vmem_limit_bytes=VMEM_LIMIT)


def _dot(a, b):
    return jnp.dot(a, b, preferred_element_type=F32)


def _dot_nt(a, b):
    return lax.dot_general(a, b, (((1,), (1,)), ((), ())), preferred_element_type=F32)


def _ln_kernel(x_ref, g_ref, b_ref, of_ref, ob_ref):
    x = x_ref[...]
    mu = jnp.mean(x, axis=-1, keepdims=True)
    xc = x - mu
    var = jnp.mean(xc * xc, axis=-1, keepdims=True)
    y = xc * lax.rsqrt(var + LN_EPS) * g_ref[...] + b_ref[...]
    of_ref[...] = y
    ob_ref[...] = y.astype(BF16)


def _layer_norm(x, g, b, *, tr):
    rows, d = x.shape
    return pl.pallas_call(
        _ln_kernel,
        grid=(rows // tr,),
        in_specs=[pl.BlockSpec((tr, d), lambda i: (i, 0)),
                  pl.BlockSpec((1, d), lambda i: (0, 0)),
                  pl.BlockSpec((1, d), lambda i: (0, 0))],
        out_specs=[pl.BlockSpec((tr, d), lambda i: (i, 0)),
                   pl.BlockSpec((tr, d), lambda i: (i, 0))],
        out_shape=[jax.ShapeDtypeStruct((rows, d), F32), jax.ShapeDtypeStruct((rows, d), BF16)],
        compiler_params=_params("parallel"),
        name="layer_norm",
    )(x, g.reshape(1, d), b.reshape(1, d))


def _ln_final_kernel(x_ref, g_ref, b_ref, o_ref):
    x = x_ref[...]
    mu = jnp.mean(x, axis=-1, keepdims=True)
    xc = x - mu
    var = jnp.mean(xc * xc, axis=-1, keepdims=True)
    o_ref[...] = xc * lax.rsqrt(var + LN_EPS) * g_ref[...] + b_ref[...]


def _layer_norm_drop_meta(x, g, b, *, bsz, seq, tr):
    d = x.shape[1]
    real = seq - N_META
    steps = real // tr
    out = pl.pallas_call(
        _ln_final_kernel,
        grid=(bsz, steps),
        in_specs=[pl.BlockSpec((pl.Element(tr), pl.Element(d)), lambda i, j: (8 * (i * (seq // 8) + N_META // 8 + j * (tr // 8)), 0)),
                  pl.BlockSpec((1, d), lambda i, j: (0, 0)),
                  pl.BlockSpec((1, d), lambda i, j: (0, 0))],
        out_specs=pl.BlockSpec((tr, d), lambda i, j: (i * steps + j, 0)),
        out_shape=jax.ShapeDtypeStruct((bsz * real, d), F32),
        compiler_params=_params("parallel", "parallel"),
        name="layer_norm_out",
    )(x, g.reshape(1, d), b.reshape(1, d))
    return out.reshape(bsz, real, d)


def _cast_weights_once(w_refs, w_scr):
    @pl.when(pl.program_id(1) == 0)
    def _():
        col = 0
        for w_ref in w_refs:
            w_scr[:, col:col + w_ref.shape[1]] = w_ref[...].astype(BF16)
            col += w_ref.shape[1]


def _mm_kernel(x_ref, w_ref, o_ref, w_scr):
    _cast_weights_once([w_ref], w_scr)
    o_ref[...] = _dot(x_ref[...], w_scr[...]).astype(o_ref.dtype)


def _project(x, w, layer, *, col_off, n_cols, tm, tn, out_dtype):
    m, k = x.shape
    off = col_off // tn
    return pl.pallas_call(
        _mm_kernel,
        grid=(n_cols // tn, m // tm),
        in_specs=[pl.BlockSpec((tm, k), lambda j, i: (i, 0)),
                  pl.BlockSpec((None, k, tn), lambda j, i: (layer, 0, j + off))],
        out_specs=pl.BlockSpec((tm, tn), lambda j, i: (i, j)),
        out_shape=jax.ShapeDtypeStruct((m, n_cols), out_dtype),
        scratch_shapes=[pltpu.VMEM((k, tn), BF16)],
        compiler_params=_params("arbitrary", "arbitrary"),
        name="proj_attn",
    )(x, w)


def _causal_conv3(u, w, scr_ref, halo_ref, first_tile):
    tm = u.shape[0]
    scr_ref[0:CONV_HALO, :] = jnp.where(first_tile, 0.0, halo_ref[...])
    scr_ref[CONV_HALO:CONV_HALO + tm, :] = u
    halo_ref[...] = u[tm - CONV_HALO:tm, :]
    return (w[0:1, :] * scr_ref[CONV_HALO - 2:CONV_HALO - 2 + tm, :]
            + w[1:2, :] * scr_ref[CONV_HALO - 1:CONV_HALO - 1 + tm, :]
            + w[2:3, :] * u)


def _conv_mixer_kernel(x_ref, wb_ref, wc_ref, wh_ref, cw_ref, o_ref, w_scr, scr_ref, halo_ref, *, tiles_per_seq):
    i = pl.program_id(1)
    tn = o_ref.shape[1]
    _cast_weights_once([wb_ref, wc_ref, wh_ref], w_scr)

    @pl.when(i == 0)
    def _():
        halo_ref[...] = jnp.zeros(halo_ref.shape, F32)

    u = _dot(x_ref[...], w_scr[...])
    g = u[:, tn:2 * tn] * u[:, 2 * tn:]
    y = _causal_conv3(g, cw_ref[...], scr_ref, halo_ref, i % tiles_per_seq == 0)
    o_ref[...] = (u[:, :tn] * y).astype(o_ref.dtype)


def _conv_mixer(x, w_in, conv_w, layer, *, width, seq, tm, tn):
    m, k = x.shape
    nt = width // tn
    kern = functools.partial(_conv_mixer_kernel, tiles_per_seq=seq // tm)
    wspec = lambda s: pl.BlockSpec((None, k, tn), lambda j, i: (layer, 0, j + s * nt))
    return pl.pallas_call(
        kern,
        grid=(nt, m // tm),
        in_specs=[pl.BlockSpec((tm, k), lambda j, i: (i, 0)), wspec(0), wspec(1), wspec(2),
                  pl.BlockSpec((None, 3, tn), lambda j, i: (layer, 0, j))],
        out_specs=pl.BlockSpec((tm, tn), lambda j, i: (i, j)),
        out_shape=jax.ShapeDtypeStruct((m, width), BF16),
        scratch_shapes=[pltpu.VMEM((k, 3 * tn), BF16), pltpu.VMEM((CONV_HALO + tm, tn), F32),
                        pltpu.VMEM((CONV_HALO, tn), F32)],
        compiler_params=_params("arbitrary", "arbitrary"),
        name="conv_mixer",
    )(x, w_in, w_in, w_in, conv_w)


def _ffn_up_kernel(x_ref, wg_ref, wu_ref, cg_ref, cu_ref, o_ref, w_scr, scr_ref, halo_ref, *, sub, subs_per_seq):
    i = pl.program_id(1)
    tn = o_ref.shape[1]
    ts = o_ref.shape[0] // sub
    _cast_weights_once([wg_ref, wu_ref], w_scr)

    @pl.when(i == 0)
    def _():
        halo_ref[...] = jnp.zeros(halo_ref.shape, F32)

    cw = jnp.concatenate([cg_ref[...], cu_ref[...]], axis=1)
    for s in range(sub):
        rows = slice(s * ts, (s + 1) * ts)
        u = _dot(x_ref[rows, :], w_scr[...])
        c = _causal_conv3(u, cw, scr_ref.at[s], halo_ref, (i * sub + s) % subs_per_seq == 0)
        gate, up = c[:, :tn], c[:, tn:]
        o_ref[rows, :] = (gate * jax.nn.sigmoid(gate) * up).astype(o_ref.dtype)


def _ffn_up(x, w_up, conv_w, layer, *, d_ff, seq, tm, tn, sub):
    m, k = x.shape
    nt = d_ff // tn
    kern = functools.partial(_ffn_up_kernel, sub=sub, subs_per_seq=seq * sub // tm)
    return pl.pallas_call(
        kern,
        grid=(nt, m // tm),
        in_specs=[pl.BlockSpec((tm, k), lambda j, i: (i, 0)),
                  pl.BlockSpec((None, k, tn), lambda j, i: (layer, 0, j)),
                  pl.BlockSpec((None, k, tn), lambda j, i: (layer, 0, j + nt)),
                  pl.BlockSpec((None, 3, tn), lambda j, i: (layer, 0, j)),
                  pl.BlockSpec((None, 3, tn), lambda j, i: (layer, 0, j + nt))],
        out_specs=pl.BlockSpec((tm, tn), lambda j, i: (i, j)),
        out_shape=jax.ShapeDtypeStruct((m, d_ff), BF16),
        scratch_shapes=[pltpu.VMEM((k, 2 * tn), BF16), pltpu.VMEM((sub, CONV_HALO + tm // sub, 2 * tn), F32),
                        pltpu.VMEM((CONV_HALO, 2 * tn), F32)],
        compiler_params=_params("arbitrary", "arbitrary"),
        name="ffn_up",
    )(x, w_up, w_up, conv_w, conv_w)


def _out_proj_kernel(yc_ref, yd_ref, ys_ref, w_ref, h_ref, o_ref, *, alpha):
    r1 = yc_ref.shape[1]
    r2 = r1 + yd_ref.shape[1]
    mix = (_dot(yc_ref[...], w_ref[0:r1, :]) + _dot(yd_ref[...], w_ref[r1:r2, :])
           + _dot(ys_ref[...], w_ref[r2:, :]))
    o_ref[...] = alpha * h_ref[...] + mix


def _out_proj(yc, yd, ys, w, h, layer, *, alpha, tm, tn):
    m, d = h.shape
    lhs = lambda a: pl.BlockSpec((tm, a.shape[1]), lambda i, j: (i, 0))
    return pl.pallas_call(
        functools.partial(_out_proj_kernel, alpha=alpha),
        grid=(m // tm, d // tn),
        in_specs=[lhs(yc), lhs(yd), lhs(ys),
                  pl.BlockSpec((None, w.shape[1], tn), lambda i, j: (layer, 0, j)),
                  pl.BlockSpec((tm, tn), lambda i, j: (i, j))],
        out_specs=pl.BlockSpec((tm, tn), lambda i, j: (i, j)),
        out_shape=jax.ShapeDtypeStruct((m, d), F32),
        compiler_params=_params("parallel", "arbitrary"),
        name="out_proj",
    )(yc, yd, ys, w, h)


def _mm_res_kernel(x_ref, w_ref, h_ref, o_ref, *, alpha):
    o_ref[...] = alpha * h_ref[...] + _dot(x_ref[...], w_ref[...])


def _ffn_down(a, w, h, layer, *, alpha, tm, tn):
    m, d = h.shape
    k = a.shape[1]
    return pl.pallas_call(
        functools.partial(_mm_res_kernel, alpha=alpha),
        grid=(m // tm, d // tn),
        in_specs=[pl.BlockSpec((tm, k), lambda i, j: (i, 0)),
                  pl.BlockSpec((None, k, tn), lambda i, j: (layer, 0, j)),
                  pl.BlockSpec((tm, tn), lambda i, j: (i, j))],
        out_specs=pl.BlockSpec((tm, tn), lambda i, j: (i, j)),
        out_shape=jax.ShapeDtypeStruct((m, d), F32),
        compiler_params=_params("parallel", "arbitrary"),
        name="ffn_down",
    )(a, w, h)


def _pad_rows(a, rows):
    if a.shape[0] == rows:
        return a
    return jnp.concatenate([a, jnp.zeros((rows - a.shape[0], a.shape[1]), a.dtype)], axis=0)


def _head_cols(h):
    return slice(h * HEAD_DIM, (h + 1) * HEAD_DIM)


def _num_key_blocks(seq):
    return -(-seq // ATT_BLOCK)


def _stage_values_transposed(v_ref, vt_ref, seq, nh):
    for h in range(nh):
        for j in range(_num_key_blocks(seq)):
            r0 = j * ATT_BLOCK
            v = _pad_rows(v_ref[r0:min(r0 + ATT_BLOCK, seq), _head_cols(h)], ATT_BLOCK)
            vt_ref[h, j, 0:HEAD_DIM, :] = v.astype(F32).T.astype(BF16)


def _row_block(ref, j, t, nh):
    r0 = j * ATT_BLOCK
    if not isinstance(j, int):
        r0 = pl.multiple_of(r0, ATT_BLOCK)
    return [_pad_rows(ref[pl.ds(r0, t), _head_cols(h)], ATT_BLOCK) for h in range(nh)]


def _causal_sweep(seq, q_block):
    n_full, tail = seq // ATT_BLOCK, seq % ATT_BLOCK

    def full_block(i, _):
        q_block(i, ATT_BLOCK, ATT_BLOCK)
        return 0

    lax.fori_loop(0, n_full, full_block, 0)
    if tail:
        q_block(n_full, tail, ATT_TAIL_LANES)


def _key_query_iota(tp, maps):
    r = lax.broadcasted_iota(jnp.int32, (ATT_BLOCK, maps * tp), 0)
    c = lax.broadcasted_iota(jnp.int32, (ATT_BLOCK, maps * tp), 1)
    return r, (jnp.where(c >= tp, c - tp, c) if maps == 2 else c)


def _diff_attn_kernel(lam_ref, q_ref, k_ref, v_ref, g_ref, o_ref, vt_ref, *, seq, gain, nh):
    blk = ATT_BLOCK
    heads = range(nh)
    lam = lam_ref[0]
    g = g_ref[...]
    lo_lane = lax.broadcasted_iota(jnp.int32, (1, HEAD_DIM), 1) < DIFF_QK_DIM
    _stage_values_transposed(v_ref, vt_ref, seq, nh)
    for h in heads:
        for j in range(_num_key_blocks(seq)):
            vt_ref[h, j, HEAD_DIM:, :] = jnp.ones((ONES_ROWS, blk), BF16)

    def stacked_q(h, q0, t, tp):
        q = (q_ref[pl.ds(q0, t), _head_cols(h)].astype(F32) * (DIFF_QK_DIM ** -0.5)).astype(BF16)
        q = _pad_rows(q, tp)
        zero = jnp.zeros_like(q)
        return jnp.concatenate([jnp.where(lo_lane, q, zero), jnp.where(lo_lane, zero, q)], axis=0)

    def block_step(qs, ks, j, mask, ms, accs):
        ss = [_dot_nt(ks[h], qs[h]) for h in heads]
        if mask is not None:
            ss = [jnp.where(mask, s, -jnp.inf) for s in ss]
        m_new = [jnp.maximum(ms[h], jnp.max(ss[h], axis=0, keepdims=True)) for h in heads]
        ps = [jnp.exp(ss[h] - m_new[h]).astype(BF16) for h in heads]
        pvs = [_dot(vt_ref[h, j], ps[h]) for h in heads]
        accs = [jnp.exp(ms[h] - m_new[h]) * accs[h] + pvs[h] for h in heads]
        return m_new, accs

    def q_block(i, t, tp):
        q0 = i * blk if isinstance(i, int) else pl.multiple_of(i * blk, blk)
        qs = [stacked_q(h, q0, t, tp) for h in heads]
        init = ((jnp.full((1, 2 * tp), -jnp.inf, F32),) * nh
                + (jnp.zeros((HEAD_DIM + ONES_ROWS, 2 * tp), F32),) * nh)

        def kv_step(j, carry):
            ms, accs = block_step(qs, _row_block(k_ref, j, blk, nh), j, None, carry[:nh], carry[nh:])
            return tuple(ms) + tuple(accs)

        carry = lax.fori_loop(0, i, kv_step, init)
        r, c = _key_query_iota(tp, 2)
        _, accs = block_step(qs, _row_block(k_ref, i, t, nh), i, r <= c, carry[:nh], carry[nh:])
        for h in heads:
            o = accs[h][:HEAD_DIM] * (1.0 / accs[h][HEAD_DIM:HEAD_DIM + 1])
            o = o[:, :tp] - lam * o[:, tp:]
            o = o * lax.rsqrt(jnp.mean(o * o, axis=0, keepdims=True) + LN_EPS)
            o = (o * g * gain).T
            o_ref[pl.ds(q0, t), _head_cols(h)] = o[:t].astype(o_ref.dtype)

    _causal_sweep(seq, q_block)


def _diff_attention(proj, lam, gain_g, *, heads, col_off, lam_init, nh):
    b, seq, _ = proj.shape
    w = nh * HEAD_DIM
    c0 = col_off // w
    spec = lambda s: pl.BlockSpec((None, seq, w), lambda i, h: (i, 0, c0 + s * (heads // nh) + h))
    return pl.pallas_call(
        functools.partial(_diff_attn_kernel, seq=seq, gain=1.0 - lam_init, nh=nh),
        grid=(b, heads // nh),
        in_specs=[pl.BlockSpec(memory_space=pltpu.SMEM), spec(0), spec(1), spec(2),
                  pl.BlockSpec((HEAD_DIM, 1), lambda i, h: (0, 0))],
        out_specs=pl.BlockSpec((None, seq, w), lambda i, h: (i, 0, h)),
        out_shape=jax.ShapeDtypeStruct((b, seq, heads * HEAD_DIM), BF16),
        scratch_shapes=[pltpu.VMEM((nh, _num_key_blocks(seq), HEAD_DIM + ONES_ROWS, ATT_BLOCK), BF16)],
        compiler_params=_params("parallel", "parallel"),
        name="diff_attn",
    )(lam, proj, proj, proj, gain_g.reshape(HEAD_DIM, 1))


def _sb_attn_kernel(q_ref, k_ref, v_ref, u_ref, o_ref, *, seq, nh):
    blk = ATT_BLOCK
    heads = range(nh)
    scale = HEAD_DIM ** -0.5

    def block_step(qs, ks, vs, mask, runs, accs):
        zs = [_dot_nt(qs[h], ks[h]) * scale for h in heads]
        sps = [jnp.maximum(z, 0.0) + jnp.log(1.0 + jnp.exp(-jnp.abs(z))) for z in zs]
        log_keeps = [-sp if mask is None else jnp.where(mask, -sp, 0.0) for sp in sps]
        his = [lk.astype(BF16) for lk in log_keeps]
        los = [(lk - hi.astype(F32)).astype(BF16) for lk, hi in zip(log_keeps, his)]
        incls = [_dot(jnp.concatenate([his[h], los[h]], axis=1), u_ref[...]) for h in heads]
        ws = [jnp.exp(zs[h] + incls[h] + runs[h]) for h in heads]
        if mask is not None:
            ws = [jnp.where(mask, w, 0.0) for w in ws]
        pvs = [_dot(ws[h].astype(BF16), vs[h]) for h in heads]
        runs = [runs[h] + jnp.sum(log_keeps[h], axis=-1, keepdims=True) for h in heads]
        accs = [pvs[h] if accs is None else accs[h] + pvs[h] for h in heads]
        return runs, accs

    def q_block(i, t, tp):
        del tp
        q0 = i * blk if isinstance(i, int) else pl.multiple_of(i * blk, blk)
        qs = [q_ref[pl.ds(q0, t), _head_cols(h)] for h in heads]
        r = lax.broadcasted_iota(jnp.int32, (t, blk), 0)
        c = lax.broadcasted_iota(jnp.int32, (t, blk), 1)
        runs, accs = block_step(qs, _row_block(k_ref, i, t, nh), _row_block(v_ref, i, t, nh), c < r,
                                [jnp.zeros((t, 1), F32)] * nh, None)

        def kv_step(jj, carry):
            j = i - 1 - jj
            runs, accs = block_step(qs, _row_block(k_ref, j, blk, nh), _row_block(v_ref, j, blk, nh), None,
                                    carry[:nh], carry[nh:])
            return tuple(runs) + tuple(accs)

        carry = lax.fori_loop(0, i, kv_step, tuple(runs) + tuple(accs))
        for h in heads:
            o_ref[pl.ds(q0, t), _head_cols(h)] = carry[nh + h].astype(o_ref.dtype)

    _causal_sweep(seq, q_block)


def _sb_attention(proj, *, heads, col_off, nh):
    b, seq, _ = proj.shape
    w = nh * HEAD_DIM
    c0 = col_off // w
    blk = ATT_BLOCK
    tri = (lax.broadcasted_iota(jnp.int32, (blk, blk), 0) >= lax.broadcasted_iota(jnp.int32, (blk, blk), 1))
    u2 = jnp.concatenate([tri, tri], axis=0).astype(BF16)
    spec = lambda s: pl.BlockSpec((None, seq, w), lambda i, h: (i, 0, c0 + s * (heads // nh) + h))
    return pl.pallas_call(
        functools.partial(_sb_attn_kernel, seq=seq, nh=nh),
        grid=(b, heads // nh),
        in_specs=[spec(0), spec(1), spec(2), pl.BlockSpec((2 * blk, blk), lambda i, h: (0, 0))],
        out_specs=pl.BlockSpec((None, seq, w), lambda i, h: (i, 0, h)),
        out_shape=jax.ShapeDtypeStruct((b, seq, heads * HEAD_DIM), BF16),
        compiler_params=_params("parallel", "parallel"),
        name="sb_attn",
    )(proj, proj, proj, u2)


def kernel(x, meta_tokens, emb_ln_g, emb_ln_b, w_in, short_conv_w, lambda_q1, lambda_k1, lambda_q2, lambda_k2,
           diff_norm_g, w_out, ln1_g, ln1_b, w_up, ffn_conv_w, w_down, ln2_g, ln2_b):
    bsz, _, d = x.shape
    depth = w_in.shape[0]
    conv_w = short_conv_w.shape[-1]
    d_ff = w_down.shape[1]
    diff_w = (w_in.shape[-1] - 3 * conv_w) // 6
    heads = diff_w // HEAD_DIM
    alpha = (2 * depth) ** 0.25

    meta = jnp.broadcast_to(meta_tokens[None].astype(x.dtype), (bsz, N_META, d))
    tokens = jnp.concatenate([meta, x], axis=1)
    seq = tokens.shape[1]
    rows = bsz * seq
    tm_seq = seq // 3
    tm_big = 2 * tm_seq
    assert seq % 3 == 0 and tm_seq % 16 == 0 and rows % tm_big == 0, "row tiling assumes seq = 3 * 16k rows"
    tr = _pick_tile(rows, (192, 176, 96, 48, 16))
    tn_wide = _pick_tile(math.gcd(6 * diff_w, d), (512, 256, 128))
    tn_conv = _pick_tile(math.gcd(conv_w, d_ff), (256, 128))
    nh = _pick_tile(heads, (ATT_HEADS_PER_STEP, 1))

    w_down_b = w_down.astype(BF16)
    w_out_b = w_out.astype(BF16)

    h, hb = _layer_norm(tokens.reshape(rows, d), emb_ln_g, emb_ln_b, tr=tr)
    for l in range(depth):
        lam_init = _lambda_init(l)
        lam = (jnp.exp(jnp.sum(lambda_q1[l] * lambda_k1[l])) - jnp.exp(jnp.sum(lambda_q2[l] * lambda_k2[l]))
               + lam_init).reshape(1).astype(F32)
        y_conv = _conv_mixer(hb, w_in, short_conv_w, l, width=conv_w, seq=seq, tm=tm_seq, tn=tn_conv)
        proj = _project(hb, w_in, l, col_off=3 * conv_w, n_cols=6 * diff_w, tm=tm_big, tn=tn_wide, out_dtype=BF16)
        proj = proj.reshape(bsz, seq, 6 * diff_w)
        y_diff = _diff_attention(proj, lam, diff_norm_g[l], heads=heads, col_off=0, lam_init=lam_init, nh=nh)
        y_sb = _sb_attention(proj, heads=heads, col_off=3 * diff_w, nh=nh)
        x1 = _out_proj(y_conv, y_diff.reshape(rows, diff_w), y_sb.reshape(rows, diff_w),
                       w_out_b, h, l, alpha=alpha, tm=tm_big, tn=tn_wide)
        h, hb = _layer_norm(x1, ln1_g[l], ln1_b[l], tr=tr)
        act = _ffn_up(hb, w_up, ffn_conv_w, l, d_ff=d_ff, seq=seq, tm=tm_big, tn=tn_conv, sub=2)
        x2 = _ffn_down(act, w_down_b, h, l, alpha=alpha, tm=tm_seq, tn=tn_wide)
        if l + 1 < depth:
            h, hb = _layer_norm(x2, ln2_g[l], ln2_b[l], tr=tr)
    return _layer_norm_drop_meta(x2, ln2_g[depth - 1], ln2_b[depth - 1], bsz=bsz, seq=seq,
                                 tr=_pick_tile(seq - N_META, (256, 128, 64, 16)))
```

```python
import functools
import math

import jax
import jax.numpy as jnp
from jax import lax
from jax.experimental import pallas as pl
from jax.experimental.pallas import tpu as pltpu

N_META = 16
HEAD_DIM = 128
DIFF_QK_DIM = HEAD_DIM // 2
LN_EPS = 1e-5
ATT_BLOCK = 256
ATT_HEADS_PER_STEP = 6
ATT_TAIL_LANES = 128
ONES_ROWS = 16
SB_DEAD_TAIL = -100.0
CONV_HALO = 8
VMEM_LIMIT = 60 * 1024 * 1024

F32 = jnp.float32
BF16 = jnp.bfloat16


def _lambda_init(layer):
    return 0.8 - 0.6 * math.exp(-0.3 * layer)


def _pick_tile(n, candidates):
    return next(c for c in candidates if n % c == 0)


def _params(*sem):
    return pltpu.CompilerParams(dimension_semantics=sem, vmem_limit_bytes=VMEM_LIMIT)


def _dot(a, b):
    return jnp.dot(a, b, preferred_element_type=F32)


def _dot_nt(a, b):
    return lax.dot_general(a, b, (((1,), (1,)), ((), ())), preferred_element_type=F32)


def _ln_kernel(x_ref, g_ref, b_ref, of_ref, ob_ref):
    x = x_ref[...]
    mu = jnp.mean(x, axis=-1, keepdims=True)
    xc = x - mu
    var = jnp.mean(xc * xc, axis=-1, keepdims=True)
    y = xc * lax.rsqrt(var + LN_EPS) * g_ref[...] + b_ref[...]
    of_ref[...] = y
    ob_ref[...] = y.astype(BF16)


def _layer_norm(x, g, b, *, tr):
    rows, d = x.shape
    return pl.pallas_call(
        _ln_kernel,
        grid=(rows // tr,),
        in_specs=[pl.BlockSpec((tr, d), lambda i: (i, 0)),
                  pl.BlockSpec((1, d), lambda i: (0, 0)),
                  pl.BlockSpec((1, d), lambda i: (0, 0))],
        out_specs=[pl.BlockSpec((tr, d), lambda i: (i, 0)),
                   pl.BlockSpec((tr, d), lambda i: (i, 0))],
        out_shape=[jax.ShapeDtypeStruct((rows, d), F32), jax.ShapeDtypeStruct((rows, d), BF16)],
        compiler_params=_params("parallel"),
        name="layer_norm",
    )(x, g.reshape(1, d), b.reshape(1, d))


def _ln_final_kernel(x_ref, g_ref, b_ref, o_ref):
    x = x_ref[...]
    mu = jnp.mean(x, axis=-1, keepdims=True)
    xc = x - mu
    var = jnp.mean(xc * xc, axis=-1, keepdims=True)
    o_ref[...] = xc * lax.rsqrt(var + LN_EPS) * g_ref[...] + b_ref[...]


def _layer_norm_drop_meta(x, g, b, *, bsz, seq, tr):
    d = x.shape[1]
    real = seq - N_META
    steps = real // tr
    out = pl.pallas_call(
        _ln_final_kernel,
        grid=(bsz, steps),
        in_specs=[pl.BlockSpec((pl.Element(tr), pl.Element(d)), lambda i, j: (8 * (i * (seq // 8) + N_META // 8 + j * (tr // 8)), 0)),
                  pl.BlockSpec((1, d), lambda i, j: (0, 0)),
                  pl.BlockSpec((1, d), lambda i, j: (0, 0))],
        out_specs=pl.BlockSpec((tr, d), lambda i, j: (i * steps + j, 0)),
        out_shape=jax.ShapeDtypeStruct((bsz * real, d), F32),
        compiler_params=_params("parallel", "parallel"),
        name="layer_norm_out",
    )(x, g.reshape(1, d), b.reshape(1, d))
    return out.reshape(bsz, real, d)


def _cast_weights_once(w_refs, w_scr):
    @pl.when(pl.program_id(1) == 0)
    def _():
        col = 0
        for w_ref in w_refs:
            w_scr[:, col:col + w_ref.shape[1]] = w_ref[...].astype(BF16)
            col += w_ref.shape[1]


def _mm_kernel(x_ref, w_ref, o_ref, w_scr):
    _cast_weights_once([w_ref], w_scr)
    o_ref[...] = _dot(x_ref[...], w_scr[...]).astype(o_ref.dtype)


def _project(x, w, layer, *, col_off, n_cols, tm, tn, out_dtype):
    m, k = x.shape
    off = col_off // tn
    return pl.pallas_call(
        _mm_kernel,
        grid=(n_cols // tn, m // tm),
        in_specs=[pl.BlockSpec((tm, k), lambda j, i: (i, 0)),
                  pl.BlockSpec((None, k, tn), lambda j, i: (layer, 0, j + off))],
        out_specs=pl.BlockSpec((tm, tn), lambda j, i: (i, j)),
        out_shape=jax.ShapeDtypeStruct((m, n_cols), out_dtype),
        scratch_shapes=[pltpu.VMEM((k, tn), BF16)],
        compiler_params=_params("arbitrary", "arbitrary"),
        name="proj_attn",
    )(x, w)


def _causal_conv3(u, w, scr_ref, halo_ref, first_tile):
    tm = u.shape[0]
    scr_ref[0:CONV_HALO, :] = jnp.where(first_tile, 0.0, halo_ref[...])
    scr_ref[CONV_HALO:CONV_HALO + tm, :] = u
    halo_ref[...] = u[tm - CONV_HALO:tm, :]
    return (w[0:1, :] * scr_ref[CONV_HALO - 2:CONV_HALO - 2 + tm, :]
            + w[1:2, :] * scr_ref[CONV_HALO - 1:CONV_HALO - 1 + tm, :]
            + w[2:3, :] * u)


def _conv_mixer_kernel(x_ref, wb_ref, wc_ref, wh_ref, cw_ref, o_ref, w_scr, scr_ref, halo_ref, *, tiles_per_seq):
    i = pl.program_id(1)
    tn = o_ref.shape[1]
    _cast_weights_once([wb_ref, wc_ref, wh_ref], w_scr)

    @pl.when(i == 0)
    def _():
        halo_ref[...] = jnp.zeros(halo_ref.shape, F32)

    u = _dot(x_ref[...], w_scr[...])
    g = u[:, tn:2 * tn] * u[:, 2 * tn:]
    y = _causal_conv3(g, cw_ref[...], scr_ref, halo_ref, i % tiles_per_seq == 0)
    o_ref[...] = (u[:, :tn] * y).astype(o_ref.dtype)


def _conv_mixer(x, w_in, conv_w, layer, *, width, seq, tm, tn):
    m, k = x.shape
    nt = width // tn
    kern = functools.partial(_conv_mixer_kernel, tiles_per_seq=seq // tm)
    wspec = lambda s: pl.BlockSpec((None, k, tn), lambda j, i: (layer, 0, j + s * nt))
    return pl.pallas_call(
        kern,
        grid=(nt, m // tm),
        in_specs=[pl.BlockSpec((tm, k), lambda j, i: (i, 0)), wspec(0), wspec(1), wspec(2),
                  pl.BlockSpec((None, 3, tn), lambda j, i: (layer, 0, j))],
        out_specs=pl.BlockSpec((tm, tn), lambda j, i: (i, j)),
        out_shape=jax.ShapeDtypeStruct((m, width), BF16),
        scratch_shapes=[pltpu.VMEM((k, 3 * tn), BF16), pltpu.VMEM((CONV_HALO + tm, tn), F32),
                        pltpu.VMEM((CONV_HALO, tn), F32)],
        compiler_params=_params("arbitrary", "arbitrary"),
        name="conv_mixer",
    )(x, w_in, w_in, w_in, conv_w)


def _ffn_up_kernel(x_ref, wg_ref, wu_ref, cg_ref, cu_ref, o_ref, w_scr, scr_ref, halo_ref, *, sub, subs_per_seq):
    i = pl.program_id(1)
    tn = o_ref.shape[1]
    ts = o_ref.shape[0] // sub
    _cast_weights_once([wg_ref, wu_ref], w_scr)

    @pl.when(i == 0)
    def _():
        halo_ref[...] = jnp.zeros(halo_ref.shape, F32)

    cw = jnp.concatenate([cg_ref[...], cu_ref[...]], axis=1)
    for s in range(sub):
        rows = slice(s * ts, (s + 1) * ts)
        u = _dot(x_ref[rows, :], w_scr[...])
        c = _causal_conv3(u, cw, scr_ref.at[s], halo_ref, (i * sub + s) % subs_per_seq == 0)
        gate, up = c[:, :tn], c[:, tn:]
        o_ref[rows, :] = (gate * jax.nn.sigmoid(gate) * up).astype(o_ref.dtype)


def _ffn_up(x, w_up, conv_w, layer, *, d_ff, seq, tm, tn, sub):
    m, k = x.shape
    nt = d_ff // tn
    kern = functools.partial(_ffn_up_kernel, sub=sub, subs_per_seq=seq * sub // tm)
    return pl.pallas_call(
        kern,
        grid=(nt, m // tm),
        in_specs=[pl.BlockSpec((tm, k), lambda j, i: (i, 0)),
                  pl.BlockSpec((None, k, tn), lambda j, i: (layer, 0, j)),
                  pl.BlockSpec((None, k, tn), lambda j, i: (layer, 0, j + nt)),
                  pl.BlockSpec((None, 3, tn), lambda j, i: (layer, 0, j)),
                  pl.BlockSpec((None, 3, tn), lambda j, i: (layer, 0, j + nt))],
        out_specs=pl.BlockSpec((tm, tn), lambda j, i: (i, j)),
        out_shape=jax.ShapeDtypeStruct((m, d_ff), BF16),
        scratch_shapes=[pltpu.VMEM((k, 2 * tn), BF16), pltpu.VMEM((sub, CONV_HALO + tm // sub, 2 * tn), F32),
                        pltpu.VMEM((CONV_HALO, 2 * tn), F32)],
        compiler_params=_params("arbitrary", "arbitrary"),
        name="ffn_up",
    )(x, w_up, w_up, conv_w, conv_w)


def _out_proj_kernel(yc_ref, yd_ref, ys_ref, w_ref, h_ref, o_ref, *, alpha):
    r1 = yc_ref.shape[1]
    r2 = r1 + yd_ref.shape[1]
    mix = (_dot(yc_ref[...], w_ref[0:r1, :]) + _dot(yd_ref[...], w_ref[r1:r2, :])
           + _dot(ys_ref[...], w_ref[r2:, :]))
    o_ref[...] = alpha * h_ref[...] + mix


def _out_proj(yc, yd, ys, w, h, layer, *, alpha, tm, tn):
    m, d = h.shape
    lhs = lambda a: pl.BlockSpec((tm, a.shape[1]), lambda i, j: (i, 0))
    return pl.pallas_call(
        functools.partial(_out_proj_kernel, alpha=alpha),
        grid=(m // tm, d // tn),
        in_specs=[lhs(yc), lhs(yd), lhs(ys),
                  pl.BlockSpec((None, w.shape[1], tn), lambda i, j: (layer, 0, j)),
                  pl.BlockSpec((tm, tn), lambda i, j: (i, j))],
        out_specs=pl.BlockSpec((tm, tn), lambda i, j: (i, j)),
        out_shape=jax.ShapeDtypeStruct((m, d), F32),
        compiler_params=_params("parallel", "arbitrary"),
        name="out_proj",
    )(yc, yd, ys, w, h)


def _mm_res_kernel(x_ref, w_ref, h_ref, o_ref, *, alpha):
    o_ref[...] = alpha * h_ref[...] + _dot(x_ref[...], w_ref[...])


def _ffn_down(a, w, h, layer, *, alpha, tm, tn):
    m, d = h.shape
    k = a.shape[1]
    return pl.pallas_call(
        functools.partial(_mm_res_kernel, alpha=alpha),
        grid=(m // tm, d // tn),
        in_specs=[pl.BlockSpec((tm, k), lambda i, j: (i, 0)),
                  pl.BlockSpec((None, k, tn), lambda i, j: (layer, 0, j)),
                  pl.BlockSpec((tm, tn), lambda i, j: (i, j))],
        out_specs=pl.BlockSpec((tm, tn), lambda i, j: (i, j)),
        out_shape=jax.ShapeDtypeStruct((m, d), F32),
        compiler_params=_params("parallel", "arbitrary"),
        name="ffn_down",
    )(a, w, h)


def _pad_rows(a, rows):
    if a.shape[0] == rows:
        return a
    return jnp.concatenate([a, jnp.zeros((rows - a.shape[0], a.shape[1]), a.dtype)], axis=0)


def _head_cols(h):
    return slice(h * HEAD_DIM, (h + 1) * HEAD_DIM)


def _num_key_blocks(seq):
    return -(-seq // ATT_BLOCK)


def _stage_values_transposed(v_ref, vt_ref, seq, nh):
    for h in range(nh):
        for j in range(_num_key_blocks(seq)):
            r0 = j * ATT_BLOCK
            v = _pad_rows(v_ref[r0:min(r0 + ATT_BLOCK, seq), _head_cols(h)], ATT_BLOCK)
            vt_ref[h, j, 0:HEAD_DIM, :] = v.astype(F32).T.astype(BF16)


def _row_block(ref, j, t, nh):
    r0 = j * ATT_BLOCK
    if not isinstance(j, int):
        r0 = pl.multiple_of(r0, ATT_BLOCK)
    return [_pad_rows(ref[pl.ds(r0, t), _head_cols(h)], ATT_BLOCK) for h in range(nh)]


def _causal_sweep(seq, q_block):
    n_full, tail = seq // ATT_BLOCK, seq % ATT_BLOCK

    def full_block(i, _):
        q_block(i, ATT_BLOCK, ATT_BLOCK)
        return 0

    lax.fori_loop(0, n_full, full_block, 0)
    if tail:
        q_block(n_full, tail, ATT_TAIL_LANES)


def _key_query_iota(tp, maps):
    r = lax.broadcasted_iota(jnp.int32, (ATT_BLOCK, maps * tp), 0)
    c = lax.broadcasted_iota(jnp.int32, (ATT_BLOCK, maps * tp), 1)
    return r, (jnp.where(c >= tp, c - tp, c) if maps == 2 else c)


def _diff_attn_kernel(lam_ref, q_ref, k_ref, v_ref, g_ref, o_ref, vt_ref, *, seq, gain, nh):
    blk = ATT_BLOCK
    heads = range(nh)
    lam = lam_ref[0]
    g = g_ref[...]
    lo_lane = lax.broadcasted_iota(jnp.int32, (1, HEAD_DIM), 1) < DIFF_QK_DIM
    _stage_values_transposed(v_ref, vt_ref, seq, nh)
    for h in heads:
        for j in range(_num_key_blocks(seq)):
            vt_ref[h, j, HEAD_DIM:, :] = jnp.ones((ONES_ROWS, blk), BF16)

    def stacked_q(h, q0, t, tp):
        q = (q_ref[pl.ds(q0, t), _head_cols(h)].astype(F32) * (DIFF_QK_DIM ** -0.5)).astype(BF16)
        q = _pad_rows(q, tp)
        zero = jnp.zeros_like(q)
        return jnp.concatenate([jnp.where(lo_lane, q, zero), jnp.where(lo_lane, zero, q)], axis=0)

    def block_step(qs, ks, j, mask, ms, accs):
        ss = [_dot_nt(ks[h], qs[h]) for h in heads]
        if mask is not None:
            ss = [jnp.where(mask, s, -jnp.inf) for s in ss]
        m_new = [jnp.maximum(ms[h], jnp.max(ss[h], axis=0, keepdims=True)) for h in heads]
        ps = [jnp.exp(ss[h] - m_new[h]).astype(BF16) for h in heads]
        pvs = [_dot(vt_ref[h, j], ps[h]) for h in heads]
        accs = [jnp.exp(ms[h] - m_new[h]) * accs[h] + pvs[h] for h in heads]
        return m_new, accs

    def q_block(i, t, tp):
        q0 = i * blk if isinstance(i, int) else pl.multiple_of(i * blk, blk)
        qs = [stacked_q(h, q0, t, tp) for h in heads]
        init = ((jnp.full((1, 2 * tp), -jnp.inf, F32),) * nh
                + (jnp.zeros((HEAD_DIM + ONES_ROWS, 2 * tp), F32),) * nh)

        def kv_step(j, carry):
            ms, accs = block_step(qs, _row_block(k_ref, j, blk, nh), j, None, carry[:nh], carry[nh:])
            return tuple(ms) + tuple(accs)

        carry = lax.fori_loop(0, i, kv_step, init)
        r, c = _key_query_iota(tp, 2)
        _, accs = block_step(qs, _row_block(k_ref, i, t, nh), i, r <= c, carry[:nh], carry[nh:])
        for h in heads:
            o = accs[h][:HEAD_DIM] * (1.0 / accs[h][HEAD_DIM:HEAD_DIM + 1])
            o = o[:, :tp] - lam * o[:, tp:]
            o = o * lax.rsqrt(jnp.mean(o * o, axis=0, keepdims=True) + LN_EPS)
            o = (o * g * gain).T
            o_ref[pl.ds(q0, t), _head_cols(h)] = o[:t].astype(o_ref.dtype)

    _causal_sweep(seq, q_block)


def _diff_attention(proj, lam, gain_g, *, heads, col_off, lam_init, nh):
    b, seq, _ = proj.shape
    w = nh * HEAD_DIM
    c0 = col_off // w
    spec = lambda s: pl.BlockSpec((None, seq, w), lambda i, h: (i, 0, c0 + s * (heads // nh) + h))
    return pl.pallas_call(
        functools.partial(_diff_attn_kernel, seq=seq, gain=1.0 - lam_init, nh=nh),
        grid=(b, heads // nh),
        in_specs=[pl.BlockSpec(memory_space=pltpu.SMEM), spec(0), spec(1), spec(2),
                  pl.BlockSpec((HEAD_DIM, 1), lambda i, h: (0, 0))],
        out_specs=pl.BlockSpec((None, seq, w), lambda i, h: (i, 0, h)),
        out_shape=jax.ShapeDtypeStruct((b, seq, heads * HEAD_DIM), BF16),
        scratch_shapes=[pltpu.VMEM((nh, _num_key_blocks(seq), HEAD_DIM + ONES_ROWS, ATT_BLOCK), BF16)],
        compiler_params=_params("parallel", "parallel"),
        name="diff_attn",
    )(lam, proj, proj, proj, gain_g.reshape(HEAD_DIM, 1))


def _sb_attn_kernel(q_ref, k_ref, v_ref, u_ref, o_ref, *, seq, nh):
    blk = ATT_BLOCK
    heads = range(nh)
    scale = HEAD_DIM ** -0.5

    def block_step(qs, ks, vs, mask, runs, accs):
        zs = [_dot_nt(qs[h], ks[h]) * scale for h in heads]
        sps = [jnp.maximum(z, 0.0) + jnp.log(1.0 + jnp.exp(-jnp.abs(z))) for z in zs]
        log_keeps = [-sp if mask is None else jnp.where(mask, -sp, 0.0) for sp in sps]
        his = [lk.astype(BF16) for lk in log_keeps]
        los = [(lk - hi.astype(F32)).astype(BF16) for lk, hi in zip(log_keeps, his)]
        incls = [_dot(jnp.concatenate([his[h], los[h]], axis=1), u_ref[...]) for h in heads]
        ws = [jnp.exp(zs[h] + incls[h] + runs[h]) for h in heads]
        if mask is not None:
            ws = [jnp.where(mask, w, 0.0) for w in ws]
        pvs = [_dot(ws[h].astype(BF16), vs[h]) for h in heads]
        runs = [runs[h] + jnp.sum(log_keeps[h], axis=-1, keepdims=True) for h in heads]
        accs = [pvs[h] if accs is None else accs[h] + pvs[h] for h in heads]
        return runs, accs

    def q_block(i, t, tp):
        del tp
        q0 = i * blk if isinstance(i, int) else pl.multiple_of(i * blk, blk)
        qs = [q_ref[pl.ds(q0, t), _head_cols(h)] for h in heads]
        r = lax.broadcasted_iota(jnp.int32, (t, blk), 0)
        c = lax.broadcasted_iota(jnp.int32, (t, blk), 1)
        runs, accs = block_step(qs, _row_block(k_ref, i, t, nh), _row_block(v_ref, i, t, nh), c < r,
                                [jnp.zeros((t, 1), F32)] * nh, None)

        def any_live(runs):
            return jnp.max(functools.reduce(jnp.maximum, runs)) > SB_DEAD_TAIL

        def kv_cond(state):
            return (state[0] < i) & state[1]

        def kv_step(state):
            jj, carry = state[0], state[2:]
            j = i - 1 - jj
            runs, accs = block_step(qs, _row_block(k_ref, j, blk, nh), _row_block(v_ref, j, blk, nh), None,
                                    carry[:nh], carry[nh:])
            return (jj + 1, any_live(runs)) + tuple(runs) + tuple(accs)

        state = lax.while_loop(kv_cond, kv_step, (jnp.int32(0), any_live(runs)) + tuple(runs) + tuple(accs))
        for h in heads:
            o_ref[pl.ds(q0, t), _head_cols(h)] = state[2 + nh + h].astype(o_ref.dtype)

    _causal_sweep(seq, q_block)


def _sb_attention(proj, *, heads, col_off, nh):
    b, seq, _ = proj.shape
    w = nh * HEAD_DIM
    c0 = col_off // w
    blk = ATT_BLOCK
    tri = (lax.broadcasted_iota(jnp.int32, (blk, blk), 0) >= lax.broadcasted_iota(jnp.int32, (blk, blk), 1))
    u2 = jnp.concatenate([tri, tri], axis=0).astype(BF16)
    spec = lambda s: pl.BlockSpec((None, seq, w), lambda i, h: (i, 0, c0 + s * (heads // nh) + h))
    return pl.pallas_call(
        functools.partial(_sb_attn_kernel, seq=seq, nh=nh),
        grid=(b, heads // nh),
        in_specs=[spec(0), spec(1), spec(2), pl.BlockSpec((2 * blk, blk), lambda i, h: (0, 0))],
        out_specs=pl.BlockSpec((None, seq, w), lambda i, h: (i, 0, h)),
        out_shape=jax.ShapeDtypeStruct((b, seq, heads * HEAD_DIM), BF16),
        compiler_params=_params("parallel", "parallel"),
        name="sb_attn",
    )(proj, proj, proj, u2)


def kernel(x, meta_tokens, emb_ln_g, emb_ln_b, w_in, short_conv_w, lambda_q1, lambda_k1, lambda_q2, lambda_k2,
           diff_norm_g, w_out, ln1_g, ln1_b, w_up, ffn_conv_w, w_down, ln2_g, ln2_b):
    bsz, _, d = x.shape
    depth = w_in.shape[0]
    conv_w = short_conv_w.shape[-1]
    d_ff = w_down.shape[1]
    diff_w = (w_in.shape[-1] - 3 * conv_w) // 6
    heads = diff_w // HEAD_DIM
    alpha = (2 * depth) ** 0.25

    meta = jnp.broadcast_to(meta_tokens[None].astype(x.dtype), (bsz, N_META, d))
    tokens = jnp.concatenate([meta, x], axis=1)
    seq = tokens.shape[1]
    rows = bsz * seq
    tm_seq = seq // 3
    tm_big = 2 * tm_seq
    assert seq % 3 == 0 and tm_seq % 16 == 0 and rows % tm_big == 0, "row tiling assumes seq = 3 * 16k rows"
    tr = _pick_tile(rows, (192, 176, 96, 48, 16))
    tn_wide = _pick_tile(math.gcd(6 * diff_w, d), (512, 256, 128))
    tn_conv = _pick_tile(math.gcd(conv_w, d_ff), (256, 128))
    nh = _pick_tile(heads, (ATT_HEADS_PER_STEP, 1))

    w_down_b = w_down.astype(BF16)
    w_out_b = w_out.astype(BF16)

    h, hb = _layer_norm(tokens.reshape(rows, d), emb_ln_g, emb_ln_b, tr=tr)
    for l in range(depth):
        lam_init = _lambda_init(l)
        lam = (jnp.exp(jnp.sum(lambda_q1[l] * lambda_k1[l])) - jnp.exp(jnp.sum(lambda_q2[l] * lambda_k2[l]))
               + lam_init).reshape(1).astype(F32)
        y_conv = _conv_mixer(hb, w_in, short_conv_w, l, width=conv_w, seq=seq, tm=tm_seq, tn=tn_conv)
        proj = _project(hb, w_in, l, col_off=3 * conv_w, n_cols=6 * diff_w, tm=tm_big, tn=tn_wide, out_dtype=BF16)
        proj = proj.reshape(bsz, seq, 6 * diff_w)
        y_diff = _diff_attention(proj, lam, diff_norm_g[l], heads=heads, col_off=0, lam_init=lam_init, nh=nh)
        y_sb = _sb_attention(proj, heads=heads, col_off=3 * diff_w, nh=nh)
        x1 = _out_proj(y_conv, y_diff.reshape(rows, diff_w), y_sb.reshape(rows, diff_w),
                       w_out_b, h, l, alpha=alpha, tm=tm_big, tn=tn_wide)
        h, hb = _layer_norm(x1, ln1_g[l], ln1_b[l], tr=tr)
        act = _ffn_up(hb, w_up, ffn_conv_w, l, d_ff=d_ff, seq=seq, tm=tm_big, tn=tn_conv, sub=2)
        x2 = _ffn_down(act, w_down_b, h, l, alpha=alpha, tm=tm_seq, tn=tn_wide)
        if l + 1 < depth:
            h, hb = _layer_norm(x2, ln2_g[l], ln2_b[l], tr=tr)
    return _layer_norm_drop_meta(x2, ln2_g[depth - 1], ln2_b[depth - 1], bsz=bsz, seq=seq,
                                 tr=_pick_tile(seq - N_META, (256, 128, 64, 16)))
```

```python
import functools
import math

import jax
import jax.numpy as jnp
from jax import lax
from jax.experimental import pallas as pl
from jax.experimental.pallas import tpu as pltpu

N_META = 16
HEAD_DIM = 128
DIFF_QK_DIM = HEAD_DIM // 2
LN_EPS = 1e-5
ATT_BLOCK = 256
ATT_HEADS_PER_STEP = 6
ATT_TAIL_LANES = 128
ONES_ROWS = 16
SB_DEAD_TAIL = -100.0
CONV_HALO = 8
VMEM_LIMIT = 60 * 1024 * 1024

F32 = jnp.float32
BF16 = jnp.bfloat16


def _lambda_init(layer):
    return 0.8 - 0.6 * math.exp(-0.3 * layer)


def _pick_tile(n, candidates):
    return next(c for c in candidates if n % c == 0)


def _params(*sem):
    return pltpu.CompilerParams(dimension_semantics=sem, vmem_limit_bytes=VMEM_LIMIT)


def _dot(a, b):
    return jnp.dot(a, b, preferred_element_type=F32)


def _dot_nt(a, b):
    return lax.dot_general(a, b, (((1,), (1,)), ((), ())), preferred_element_type=F32)


def _ln_kernel(x_ref, g_ref, b_ref, of_ref, ob_ref):
    x = x_ref[...]
    mu = jnp.mean(x, axis=-1, keepdims=True)
    xc = x - mu
    var = jnp.mean(xc * xc, axis=-1, keepdims=True)
    y = xc * lax.rsqrt(var + LN_EPS) * g_ref[...] + b_ref[...]
    of_ref[...] = y
    ob_ref[...] = y.astype(BF16)


def _layer_norm(x, g, b, *, tr):
    rows, d = x.shape
    return pl.pallas_call(
        _ln_kernel,
        grid=(rows // tr,),
        in_specs=[pl.BlockSpec((tr, d), lambda i: (i, 0)),
                  pl.BlockSpec((1, d), lambda i: (0, 0)),
                  pl.BlockSpec((1, d), lambda i: (0, 0))],
        out_specs=[pl.BlockSpec((tr, d), lambda i: (i, 0)),
                   pl.BlockSpec((tr, d), lambda i: (i, 0))],
        out_shape=[jax.ShapeDtypeStruct((rows, d), F32), jax.ShapeDtypeStruct((rows, d), BF16)],
        compiler_params=_params("parallel"),
        name="layer_norm",
    )(x, g.reshape(1, d), b.reshape(1, d))


def _ln_final_kernel(x_ref, g_ref, b_ref, o_ref):
    x = x_ref[...]
    mu = jnp.mean(x, axis=-1, keepdims=True)
    xc = x - mu
    var = jnp.mean(xc * xc, axis=-1, keepdims=True)
    o_ref[...] = xc * lax.rsqrt(var + LN_EPS) * g_ref[...] + b_ref[...]


def _layer_norm_drop_meta(x, g, b, *, bsz, seq, tr):
    d = x.shape[1]
    real = seq - N_META
    steps = real // tr
    out = pl.pallas_call(
        _ln_final_kernel,
        grid=(bsz, steps),
        in_specs=[pl.BlockSpec((pl.Element(tr), pl.Element(d)), lambda i, j: (8 * (i * (seq // 8) + N_META // 8 + j * (tr // 8)), 0)),
                  pl.BlockSpec((1, d), lambda i, j: (0, 0)),
                  pl.BlockSpec((1, d), lambda i, j: (0, 0))],
        out_specs=pl.BlockSpec((tr, d), lambda i, j: (i * steps + j, 0)),
        out_shape=jax.ShapeDtypeStruct((bsz * real, d), F32),
        compiler_params=_params("parallel", "parallel"),
        name="layer_norm_out",
    )(x, g.reshape(1, d), b.reshape(1, d))
    return out.reshape(bsz, real, d)


def _cast_weights_once(w_refs, w_scr):
    @pl.when(pl.program_id(1) == 0)
    def _():
        col = 0
        for w_ref in w_refs:
            w_scr[:, col:col + w_ref.shape[1]] = w_ref[...].astype(BF16)
            col += w_ref.shape[1]


def _mm_kernel(x_ref, w_ref, o_ref, w_scr):
    _cast_weights_once([w_ref], w_scr)
    o_ref[...] = _dot(x_ref[...], w_scr[...]).astype(o_ref.dtype)


def _project(x, w, layer, *, col_off, n_cols, tm, tn, out_dtype):
    m, k = x.shape
    off = col_off // tn
    return pl.pallas_call(
        _mm_kernel,
        grid=(n_cols // tn, m // tm),
        in_specs=[pl.BlockSpec((tm, k), lambda j, i: (i, 0)),
                  pl.BlockSpec((None, k, tn), lambda j, i: (layer, 0, j + off))],
        out_specs=pl.BlockSpec((tm, tn), lambda j, i: (i, j)),
        out_shape=jax.ShapeDtypeStruct((m, n_cols), out_dtype),
        scratch_shapes=[pltpu.VMEM((k, tn), BF16)],
        compiler_params=_params("arbitrary", "arbitrary"),
        name="proj_attn",
    )(x, w)


def _causal_conv3(u, w, scr_ref, halo_ref, first_tile):
    tm = u.shape[0]
    scr_ref[0:CONV_HALO, :] = jnp.where(first_tile, 0.0, halo_ref[...])
    scr_ref[CONV_HALO:CONV_HALO + tm, :] = u
    halo_ref[...] = u[tm - CONV_HALO:tm, :]
    return (w[0:1, :] * scr_ref[CONV_HALO - 2:CONV_HALO - 2 + tm, :]
            + w[1:2, :] * scr_ref[CONV_HALO - 1:CONV_HALO - 1 + tm, :]
            + w[2:3, :] * u)


def _conv_mixer_kernel(x_ref, wb_ref, wc_ref, wh_ref, cw_ref, o_ref, w_scr, scr_ref, halo_ref, *, tiles_per_seq):
    i = pl.program_id(1)
    tn = o_ref.shape[1]
    _cast_weights_once([wb_ref, wc_ref, wh_ref], w_scr)

    @pl.when(i == 0)
    def _():
        halo_ref[...] = jnp.zeros(halo_ref.shape, F32)

    u = _dot(x_ref[...], w_scr[...])
    g = u[:, tn:2 * tn] * u[:, 2 * tn:]
    y = _causal_conv3(g, cw_ref[...], scr_ref, halo_ref, i % tiles_per_seq == 0)
    o_ref[...] = (u[:, :tn] * y).astype(o_ref.dtype)


def _conv_mixer(x, w_in, conv_w, layer, *, width, seq, tm, tn):
    m, k = x.shape
    nt = width // tn
    kern = functools.partial(_conv_mixer_kernel, tiles_per_seq=seq // tm)
    wspec = lambda s: pl.BlockSpec((None, k, tn), lambda j, i: (layer, 0, j + s * nt))
    return pl.pallas_call(
        kern,
        grid=(nt, m // tm),
        in_specs=[pl.BlockSpec((tm, k), lambda j, i: (i, 0)), wspec(0), wspec(1), wspec(2),
                  pl.BlockSpec((None, 3, tn), lambda j, i: (layer, 0, j))],
        out_specs=pl.BlockSpec((tm, tn), lambda j, i: (i, j)),
        out_shape=jax.ShapeDtypeStruct((m, width), BF16),
        scratch_shapes=[pltpu.VMEM((k, 3 * tn), BF16), pltpu.VMEM((CONV_HALO + tm, tn), F32),
                        pltpu.VMEM((CONV_HALO, tn), F32)],
        compiler_params=_params("arbitrary", "arbitrary"),
        name="conv_mixer",
    )(x, w_in, w_in, w_in, conv_w)


def _ffn_up_kernel(x_ref, wg_ref, wu_ref, cg_ref, cu_ref, o_ref, w_scr, scr_ref, halo_ref, *, sub, subs_per_seq):
    i = pl.program_id(1)
    tn = o_ref.shape[1]
    ts = o_ref.shape[0] // sub
    _cast_weights_once([wg_ref, wu_ref], w_scr)

    @pl.when(i == 0)
    def _():
        halo_ref[...] = jnp.zeros(halo_ref.shape, F32)

    cw = jnp.concatenate([cg_ref[...], cu_ref[...]], axis=1)
    for s in range(sub):
        rows = slice(s * ts, (s + 1) * ts)
        u = _dot(x_ref[rows, :], w_scr[...])
        c = _causal_conv3(u, cw, scr_ref.at[s], halo_ref, (i * sub + s) % subs_per_seq == 0)
        gate, up = c[:, :tn], c[:, tn:]
        o_ref[rows, :] = (gate * jax.nn.sigmoid(gate) * up).astype(o_ref.dtype)


def _ffn_up(x, w_up, conv_w, layer, *, d_ff, seq, tm, tn, sub):
    m, k = x.shape
    nt = d_ff // tn
    kern = functools.partial(_ffn_up_kernel, sub=sub, subs_per_seq=seq * sub // tm)
    return pl.pallas_call(
        kern,
        grid=(nt, m // tm),
        in_specs=[pl.BlockSpec((tm, k), lambda j, i: (i, 0)),
                  pl.BlockSpec((None, k, tn), lambda j, i: (layer, 0, j)),
                  pl.BlockSpec((None, k, tn), lambda j, i: (layer, 0, j + nt)),
                  pl.BlockSpec((None, 3, tn), lambda j, i: (layer, 0, j)),
                  pl.BlockSpec((None, 3, tn), lambda j, i: (layer, 0, j + nt))],
        out_specs=pl.BlockSpec((tm, tn), lambda j, i: (i, j)),
        out_shape=jax.ShapeDtypeStruct((m, d_ff), BF16),
        scratch_shapes=[pltpu.VMEM((k, 2 * tn), BF16), pltpu.VMEM((sub, CONV_HALO + tm // sub, 2 * tn), F32),
                        pltpu.VMEM((CONV_HALO, 2 * tn), F32)],
        compiler_params=_params("arbitrary", "arbitrary"),
        name="ffn_up",
    )(x, w_up, w_up, conv_w, conv_w)


def _out_proj_kernel(yc_ref, yd_ref, ys_ref, w_ref, h_ref, o_ref, *, alpha):
    r1 = yc_ref.shape[1]
    r2 = r1 + yd_ref.shape[1]
    mix = (_dot(yc_ref[...], w_ref[0:r1, :]) + _dot(yd_ref[...], w_ref[r1:r2, :])
           + _dot(ys_ref[...], w_ref[r2:, :]))
    o_ref[...] = alpha * h_ref[...] + mix


def _out_proj(yc, yd, ys, w, h, layer, *, alpha, tm, tn):
    m, d = h.shape
    lhs = lambda a: pl.BlockSpec((tm, a.shape[1]), lambda i, j: (i, 0))
    return pl.pallas_call(
        functools.partial(_out_proj_kernel, alpha=alpha),
        grid=(m // tm, d // tn),
        in_specs=[lhs(yc), lhs(yd), lhs(ys),
                  pl.BlockSpec((None, w.shape[1], tn), lambda i, j: (layer, 0, j)),
                  pl.BlockSpec((tm, tn), lambda i, j: (i, j))],
        out_specs=pl.BlockSpec((tm, tn), lambda i, j: (i, j)),
        out_shape=jax.ShapeDtypeStruct((m, d), F32),
        compiler_params=_params("parallel", "arbitrary"),
        name="out_proj",
    )(yc, yd, ys, w, h)


def _mm_res_kernel(x_ref, w_ref, h_ref, o_ref, *, alpha):
    o_ref[...] = alpha * h_ref[...] + _dot(x_ref[...], w_ref[...])


def _ffn_down(a, w, h, layer, *, alpha, tm, tn):
    m, d = h.shape
    k = a.shape[1]
    return pl.pallas_call(
        functools.partial(_mm_res_kernel, alpha=alpha),
        grid=(m // tm, d // tn),
        in_specs=[pl.BlockSpec((tm, k), lambda i, j: (i, 0)),
                  pl.BlockSpec((None, k, tn), lambda i, j: (layer, 0, j)),
                  pl.BlockSpec((tm, tn), lambda i, j: (i, j))],
        out_specs=pl.BlockSpec((tm, tn), lambda i, j: (i, j)),
        out_shape=jax.ShapeDtypeStruct((m, d), F32),
        compiler_params=_params("parallel", "arbitrary"),
        name="ffn_down",
    )(a, w, h)


def _pad_rows(a, rows):
    if a.shape[0] == rows:
        return a
    return jnp.concatenate([a, jnp.zeros((rows - a.shape[0], a.shape[1]), a.dtype)], axis=0)


def _head_cols(h):
    return slice(h * HEAD_DIM, (h + 1) * HEAD_DIM)


def _num_key_blocks(seq):
    return -(-seq // ATT_BLOCK)


def _stage_values_transposed(v_ref, vt_ref, seq, nh):
    for h in range(nh):
        for j in range(_num_key_blocks(seq)):
            r0 = j * ATT_BLOCK
            v = _pad_rows(v_ref[r0:min(r0 + ATT_BLOCK, seq), _head_cols(h)], ATT_BLOCK)
            vt_ref[h, j, 0:HEAD_DIM, :] = v.astype(F32).T.astype(BF16)


def _row_block(ref, j, t, nh):
    r0 = j * ATT_BLOCK
    if not isinstance(j, int):
        r0 = pl.multiple_of(r0, ATT_BLOCK)
    return [_pad_rows(ref[pl.ds(r0, t), _head_cols(h)], ATT_BLOCK) for h in range(nh)]


def _causal_sweep(seq, q_block):
    n_full, tail = seq // ATT_BLOCK, seq % ATT_BLOCK

    def full_block(i, _):
        q_block(i, ATT_BLOCK, ATT_BLOCK)
        return 0

    lax.fori_loop(0, n_full, full_block, 0)
    if tail:
        q_block(n_full, tail, ATT_TAIL_LANES)


def _key_query_iota(tp, maps):
    r = lax.broadcasted_iota(jnp.int32, (ATT_BLOCK, maps * tp), 0)
    c = lax.broadcasted_iota(jnp.int32, (ATT_BLOCK, maps * tp), 1)
    return r, (jnp.where(c >= tp, c - tp, c) if maps == 2 else c)


def _diff_attn_kernel(lam_ref, q_ref, k_ref, v_ref, g_ref, o_ref, vt_ref, acc_ref, *, seq, gain, nh):
    blk = ATT_BLOCK
    heads = range(nh)
    lam = lam_ref[0]
    g = g_ref[...]
    lo_lane = lax.broadcasted_iota(jnp.int32, (1, HEAD_DIM), 1) < DIFF_QK_DIM
    _stage_values_transposed(v_ref, vt_ref, seq, nh)
    for h in heads:
        for j in range(_num_key_blocks(seq)):
            vt_ref[h, j, HEAD_DIM:, :] = jnp.ones((ONES_ROWS, blk), BF16)

    def stacked_q(h, q0, t, tp):
        q = (q_ref[pl.ds(q0, t), _head_cols(h)].astype(F32) * (DIFF_QK_DIM ** -0.5)).astype(BF16)
        q = _pad_rows(q, tp)
        zero = jnp.zeros_like(q)
        return jnp.concatenate([jnp.where(lo_lane, q, zero), jnp.where(lo_lane, zero, q)], axis=0)

    def block_step(qs, ks, j, mask, ms):
        w = qs[0].shape[0]
        ss = [_dot_nt(ks[h], qs[h]) for h in heads]
        if mask is not None:
            ss = [jnp.where(mask, s, -jnp.inf) for s in ss]
        m_new = [jnp.maximum(ms[h], jnp.max(ss[h], axis=0, keepdims=True)) for h in heads]
        ps = [jnp.exp(ss[h] - m_new[h]).astype(BF16) for h in heads]
        pvs = [_dot(vt_ref[h, j], ps[h]) for h in heads]
        for h in heads:
            acc_ref[h, :, 0:w] = jnp.exp(ms[h] - m_new[h]) * acc_ref[h, :, 0:w] + pvs[h]
        return m_new

    def q_block(i, t, tp):
        q0 = i * blk if isinstance(i, int) else pl.multiple_of(i * blk, blk)
        qs = [stacked_q(h, q0, t, tp) for h in heads]
        for h in heads:
            acc_ref[h, :, 0:2 * tp] = jnp.zeros((HEAD_DIM + ONES_ROWS, 2 * tp), F32)

        def kv_step(j, ms):
            return tuple(block_step(qs, _row_block(k_ref, j, blk, nh), j, None, ms))

        ms = lax.fori_loop(0, i, kv_step, (jnp.full((1, 2 * tp), -jnp.inf, F32),) * nh)
        r, c = _key_query_iota(tp, 2)
        block_step(qs, _row_block(k_ref, i, t, nh), i, r <= c, ms)
        for h in heads:
            acc = acc_ref[h, :, 0:2 * tp]
            o = acc[:HEAD_DIM] * (1.0 / acc[HEAD_DIM:HEAD_DIM + 1])
            o = o[:, :tp] - lam * o[:, tp:]
            o = o * lax.rsqrt(jnp.mean(o * o, axis=0, keepdims=True) + LN_EPS)
            o = (o * g * gain).T
            o_ref[pl.ds(q0, t), _head_cols(h)] = o[:t].astype(o_ref.dtype)

    _causal_sweep(seq, q_block)


def _diff_attention(proj, lam, gain_g, *, heads, col_off, lam_init, nh):
    b, seq, _ = proj.shape
    w = nh * HEAD_DIM
    c0 = col_off // w
    spec = lambda s: pl.BlockSpec((None, seq, w), lambda i, h: (i, 0, c0 + s * (heads // nh) + h))
    return pl.pallas_call(
        functools.partial(_diff_attn_kernel, seq=seq, gain=1.0 - lam_init, nh=nh),
        grid=(b, heads // nh),
        in_specs=[pl.BlockSpec(memory_space=pltpu.SMEM), spec(0), spec(1), spec(2),
                  pl.BlockSpec((HEAD_DIM, 1), lambda i, h: (0, 0))],
        out_specs=pl.BlockSpec((None, seq, w), lambda i, h: (i, 0, h)),
        out_shape=jax.ShapeDtypeStruct((b, seq, heads * HEAD_DIM), BF16),
        scratch_shapes=[pltpu.VMEM((nh, _num_key_blocks(seq), HEAD_DIM + ONES_ROWS, ATT_BLOCK), BF16),
                        pltpu.VMEM((nh, HEAD_DIM + ONES_ROWS, 2 * ATT_BLOCK), F32)],
        compiler_params=_params("parallel", "parallel"),
        name="diff_attn",
    )(lam, proj, proj, proj, gain_g.reshape(HEAD_DIM, 1))


def _sb_attn_kernel(q_ref, k_ref, v_ref, u_ref, o_ref, *, seq, nh):
    blk = ATT_BLOCK
    heads = range(nh)
    scale = HEAD_DIM ** -0.5

    def block_step(qs, ks, vs, mask, runs, accs):
        zs = [_dot_nt(qs[h], ks[h]) * scale for h in heads]
        sps = [jnp.maximum(z, 0.0) + jnp.log(1.0 + jnp.exp(-jnp.abs(z))) for z in zs]
        log_keeps = [-sp if mask is None else jnp.where(mask, -sp, 0.0) for sp in sps]
        his = [lk.astype(BF16) for lk in log_keeps]
        los = [(lk - hi.astype(F32)).astype(BF16) for lk, hi in zip(log_keeps, his)]
        incls = [_dot(jnp.concatenate([his[h], los[h]], axis=1), u_ref[...]) for h in heads]
        ws = [jnp.exp(zs[h] + incls[h] + runs[h]) for h in heads]
        if mask is not None:
            ws = [jnp.where(mask, w, 0.0) for w in ws]
        pvs = [_dot(ws[h].astype(BF16), vs[h]) for h in heads]
        runs = [runs[h] + jnp.sum(log_keeps[h], axis=-1, keepdims=True) for h in heads]
        accs = [pvs[h] if accs is None else accs[h] + pvs[h] for h in heads]
        return runs, accs

    def q_block(i, t, tp):
        del tp
        q0 = i * blk if isinstance(i, int) else pl.multiple_of(i * blk, blk)
        qs = [q_ref[pl.ds(q0, t), _head_cols(h)] for h in heads]
        r = lax.broadcasted_iota(jnp.int32, (t, blk), 0)
        c = lax.broadcasted_iota(jnp.int32, (t, blk), 1)
        runs, accs = block_step(qs, _row_block(k_ref, i, t, nh), _row_block(v_ref, i, t, nh), c < r,
                                [jnp.zeros((t, 1), F32)] * nh, None)

        def any_live(runs):
            return jnp.max(functools.reduce(jnp.maximum, runs)) > SB_DEAD_TAIL

        def kv_cond(state):
            return (state[0] < i) & state[1]

        def kv_step(state):
            jj, carry = state[0], state[2:]
            j = i - 1 - jj
            runs, accs = block_step(qs, _row_block(k_ref, j, blk, nh), _row_block(v_ref, j, blk, nh), None,
                                    carry[:nh], carry[nh:])
            return (jj + 1, any_live(runs)) + tuple(runs) + tuple(accs)

        state = lax.while_loop(kv_cond, kv_step, (jnp.int32(0), any_live(runs)) + tuple(runs) + tuple(accs))
        for h in heads:
            o_ref[pl.ds(q0, t), _head_cols(h)] = state[2 + nh + h].astype(o_ref.dtype)

    _causal_sweep(seq, q_block)


def _sb_attention(proj, *, heads, col_off, nh):
    b, seq, _ = proj.shape
    w = nh * HEAD_DIM
    c0 = col_off // w
    blk = ATT_BLOCK
    tri = (lax.broadcasted_iota(jnp.int32, (blk, blk), 0) >= lax.broadcasted_iota(jnp.int32, (blk, blk), 1))
    u2 = jnp.concatenate([tri, tri], axis=0).astype(BF16)
    spec = lambda s: pl.BlockSpec((None, seq, w), lambda i, h: (i, 0, c0 + s * (heads // nh) + h))
    return pl.pallas_call(
        functools.partial(_sb_attn_kernel, seq=seq, nh=nh),
        grid=(b, heads // nh),
        in_specs=[spec(0), spec(1), spec(2), pl.BlockSpec((2 * blk, blk), lambda i, h: (0, 0))],
        out_specs=pl.BlockSpec((None, seq, w), lambda i, h: (i, 0, h)),
        out_shape=jax.ShapeDtypeStruct((b, seq, heads * HEAD_DIM), BF16),
        compiler_params=_params("parallel", "parallel"),
        name="sb_attn",
    )(proj, proj, proj, u2)


def kernel(x, meta_tokens, emb_ln_g, emb_ln_b, w_in, short_conv_w, lambda_q1, lambda_k1, lambda_q2, lambda_k2,
           diff_norm_g, w_out, ln1_g, ln1_b, w_up, ffn_conv_w, w_down, ln2_g, ln2_b):
    bsz, _, d = x.shape
    depth = w_in.shape[0]
    conv_w = short_conv_w.shape[-1]
    d_ff = w_down.shape[1]
    diff_w = (w_in.shape[-1] - 3 * conv_w) // 6
    heads = diff_w // HEAD_DIM
    alpha = (2 * depth) ** 0.25

    meta = jnp.broadcast_to(meta_tokens[None].astype(x.dtype), (bsz, N_META, d))
    tokens = jnp.concatenate([meta, x], axis=1)
    seq = tokens.shape[1]
    rows = bsz * seq
    tm_seq = seq // 3
    tm_big = 2 * tm_seq
    assert seq % 3 == 0 and tm_seq % 16 == 0 and rows % tm_big == 0, "row tiling assumes seq = 3 * 16k rows"
    tr = _pick_tile(rows, (192, 176, 96, 48, 16))
    tn_wide = _pick_tile(math.gcd(6 * diff_w, d), (512, 256, 128))
    tn_conv = _pick_tile(math.gcd(conv_w, d_ff), (256, 128))
    nh = _pick_tile(heads, (ATT_HEADS_PER_STEP, 1))

    w_down_b = w_down.astype(BF16)
    w_out_b = w_out.astype(BF16)

    h, hb = _layer_norm(tokens.reshape(rows, d), emb_ln_g, emb_ln_b, tr=tr)
    for l in range(depth):
        lam_init = _lambda_init(l)
        lam = (jnp.exp(jnp.sum(lambda_q1[l] * lambda_k1[l])) - jnp.exp(jnp.sum(lambda_q2[l] * lambda_k2[l]))
               + lam_init).reshape(1).astype(F32)
        y_conv = _conv_mixer(hb, w_in, short_conv_w, l, width=conv_w, seq=seq, tm=tm_seq, tn=tn_conv)
        proj = _project(hb, w_in, l, col_off=3 * conv_w, n_cols=6 * diff_w, tm=tm_big, tn=tn_wide, out_dtype=BF16)
        proj = proj.reshape(bsz, seq, 6 * diff_w)
        y_diff = _diff_attention(proj, lam, diff_norm_g[l], heads=heads, col_off=0, lam_init=lam_init, nh=nh)
        y_sb = _sb_attention(proj, heads=heads, col_off=3 * diff_w, nh=nh)
        x1 = _out_proj(y_conv, y_diff.reshape(rows, diff_w), y_sb.reshape(rows, diff_w),
                       w_out_b, h, l, alpha=alpha, tm=tm_big, tn=tn_wide)
        h, hb = _layer_norm(x1, ln1_g[l], ln1_b[l], tr=tr)
        act = _ffn_up(hb, w_up, ffn_conv_w, l, d_ff=d_ff, seq=seq, tm=tm_big, tn=tn_conv, sub=2)
        x2 = _ffn_down(act, w_down_b, h, l, alpha=alpha, tm=tm_seq, tn=tn_wide)
        if l + 1 < depth:
            h, hb = _layer_norm(x2, ln2_g[l], ln2_b[l], tr=tr)
    return _layer_norm_drop_meta(x2, ln2_g[depth - 1], ln2_b[depth - 1], bsz=bsz, seq=seq,
                                 tr=_pick_tile(seq - N_META, (256, 128, 64, 16)))
```

```python
import functools
import math

import jax
import jax.numpy as jnp
from jax import lax
from jax.experimental import pallas as pl
from jax.experimental.pallas import tpu as pltpu

N_META = 16
HEAD_DIM = 128
DIFF_QK_DIM = HEAD_DIM // 2
LN_EPS = 1e-5
ATT_BLOCK = 256
ATT_HEADS_PER_STEP = 6
ATT_TAIL_LANES = 64
ONES_ROWS = 16
SB_DEAD_TAIL = -100.0
CONV_HALO = 8
VMEM_LIMIT = 60 * 1024 * 1024

F32 = jnp.float32
BF16 = jnp.bfloat16


def _lambda_init(layer):
    return 0.8 - 0.6 * math.exp(-0.3 * layer)


def _pick_tile(n, candidates):
    return next(c for c in candidates if n % c == 0)


def _params(*sem):
    return pltpu.CompilerParams(dimension_semantics=sem, vmem_limit_bytes=VMEM_LIMIT)


def _dot(a, b):
    return jnp.dot(a, b, preferred_element_type=F32)


def _dot_nt(a, b):
    return lax.dot_general(a, b, (((1,), (1,)), ((), ())), preferred_element_type=F32)


def _ln_kernel(x_ref, g_ref, b_ref, of_ref, ob_ref):
    x = x_ref[...]
    mu = jnp.mean(x, axis=-1, keepdims=True)
    xc = x - mu
    var = jnp.mean(xc * xc, axis=-1, keepdims=True)
    y = xc * lax.rsqrt(var + LN_EPS) * g_ref[...] + b_ref[...]
    of_ref[...] = y
    ob_ref[...] = y.astype(BF16)


def _layer_norm(x, g, b, *, tr):
    rows, d = x.shape
    return pl.pallas_call(
        _ln_kernel,
        grid=(rows // tr,),
        in_specs=[pl.BlockSpec((tr, d), lambda i: (i, 0)),
                  pl.BlockSpec((1, d), lambda i: (0, 0)),
                  pl.BlockSpec((1, d), lambda i: (0, 0))],
        out_specs=[pl.BlockSpec((tr, d), lambda i: (i, 0)),
                   pl.BlockSpec((tr, d), lambda i: (i, 0))],
        out_shape=[jax.ShapeDtypeStruct((rows, d), F32), jax.ShapeDtypeStruct((rows, d), BF16)],
        compiler_params=_params("parallel"),
        name="layer_norm",
    )(x, g.reshape(1, d), b.reshape(1, d))


def _ln_final_kernel(x_ref, g_ref, b_ref, o_ref):
    x = x_ref[...]
    mu = jnp.mean(x, axis=-1, keepdims=True)
    xc = x - mu
    var = jnp.mean(xc * xc, axis=-1, keepdims=True)
    o_ref[...] = xc * lax.rsqrt(var + LN_EPS) * g_ref[...] + b_ref[...]


def _layer_norm_drop_meta(x, g, b, *, bsz, seq, tr):
    d = x.shape[1]
    real = seq - N_META
    steps = real // tr
    out = pl.pallas_call(
        _ln_final_kernel,
        grid=(bsz, steps),
        in_specs=[pl.BlockSpec((pl.Element(tr), pl.Element(d)), lambda i, j: (8 * (i * (seq // 8) + N_META // 8 + j * (tr // 8)), 0)),
                  pl.BlockSpec((1, d), lambda i, j: (0, 0)),
                  pl.BlockSpec((1, d), lambda i, j: (0, 0))],
        out_specs=pl.BlockSpec((tr, d), lambda i, j: (i * steps + j, 0)),
        out_shape=jax.ShapeDtypeStruct((bsz * real, d), F32),
        compiler_params=_params("parallel", "parallel"),
        name="layer_norm_out",
    )(x, g.reshape(1, d), b.reshape(1, d))
    return out.reshape(bsz, real, d)


def _cast_weights_once(w_refs, w_scr):
    @pl.when(pl.program_id(1) == 0)
    def _():
        col = 0
        for w_ref in w_refs:
            w_scr[:, col:col + w_ref.shape[1]] = w_ref[...].astype(BF16)
            col += w_ref.shape[1]


def _mm_kernel(x_ref, w_ref, o_ref, w_scr):
    _cast_weights_once([w_ref], w_scr)
    o_ref[...] = _dot(x_ref[...], w_scr[...]).astype(o_ref.dtype)


def _project(x, w, layer, *, col_off, n_cols, tm, tn, out_dtype):
    m, k = x.shape
    off = col_off // tn
    return pl.pallas_call(
        _mm_kernel,
        grid=(n_cols // tn, m // tm),
        in_specs=[pl.BlockSpec((tm, k), lambda j, i: (i, 0)),
                  pl.BlockSpec((None, k, tn), lambda j, i: (layer, 0, j + off))],
        out_specs=pl.BlockSpec((tm, tn), lambda j, i: (i, j)),
        out_shape=jax.ShapeDtypeStruct((m, n_cols), out_dtype),
        scratch_shapes=[pltpu.VMEM((k, tn), BF16)],
        compiler_params=_params("arbitrary", "arbitrary"),
        name="proj_attn",
    )(x, w)


def _causal_conv3(u, w, scr_ref, halo_ref, first_tile):
    tm = u.shape[0]
    scr_ref[0:CONV_HALO, :] = jnp.where(first_tile, 0.0, halo_ref[...])
    scr_ref[CONV_HALO:CONV_HALO + tm, :] = u
    halo_ref[...] = u[tm - CONV_HALO:tm, :]
    return (w[0:1, :] * scr_ref[CONV_HALO - 2:CONV_HALO - 2 + tm, :]
            + w[1:2, :] * scr_ref[CONV_HALO - 1:CONV_HALO - 1 + tm, :]
            + w[2:3, :] * u)


def _conv_mixer_kernel(x_ref, wb_ref, wc_ref, wh_ref, cw_ref, o_ref, w_scr, scr_ref, halo_ref, *, tiles_per_seq):
    i = pl.program_id(1)
    tn = o_ref.shape[1]
    _cast_weights_once([wb_ref, wc_ref, wh_ref], w_scr)

    @pl.when(i == 0)
    def _():
        halo_ref[...] = jnp.zeros(halo_ref.shape, F32)

    u = _dot(x_ref[...], w_scr[...])
    g = u[:, tn:2 * tn] * u[:, 2 * tn:]
    y = _causal_conv3(g, cw_ref[...], scr_ref, halo_ref, i % tiles_per_seq == 0)
    o_ref[...] = (u[:, :tn] * y).astype(o_ref.dtype)


def _conv_mixer(x, w_in, conv_w, layer, *, width, seq, tm, tn):
    m, k = x.shape
    nt = width // tn
    kern = functools.partial(_conv_mixer_kernel, tiles_per_seq=seq // tm)
    wspec = lambda s: pl.BlockSpec((None, k, tn), lambda j, i: (layer, 0, j + s * nt))
    return pl.pallas_call(
        kern,
        grid=(nt, m // tm),
        in_specs=[pl.BlockSpec((tm, k), lambda j, i: (i, 0)), wspec(0), wspec(1), wspec(2),
                  pl.BlockSpec((None, 3, tn), lambda j, i: (layer, 0, j))],
        out_specs=pl.BlockSpec((tm, tn), lambda j, i: (i, j)),
        out_shape=jax.ShapeDtypeStruct((m, width), BF16),
        scratch_shapes=[pltpu.VMEM((k, 3 * tn), BF16), pltpu.VMEM((CONV_HALO + tm, tn), F32),
                        pltpu.VMEM((CONV_HALO, tn), F32)],
        compiler_params=_params("arbitrary", "arbitrary"),
        name="conv_mixer",
    )(x, w_in, w_in, w_in, conv_w)


def _ffn_up_kernel(x_ref, wg_ref, wu_ref, cg_ref, cu_ref, o_ref, w_scr, scr_ref, halo_ref, *, sub, subs_per_seq):
    i = pl.program_id(1)
    tn = o_ref.shape[1]
    ts = o_ref.shape[0] // sub
    _cast_weights_once([wg_ref, wu_ref], w_scr)

    @pl.when(i == 0)
    def _():
        halo_ref[...] = jnp.zeros(halo_ref.shape, F32)

    cw = jnp.concatenate([cg_ref[...], cu_ref[...]], axis=1)
    for s in range(sub):
        rows = slice(s * ts, (s + 1) * ts)
        u = _dot(x_ref[rows, :], w_scr[...])
        c = _causal_conv3(u, cw, scr_ref.at[s], halo_ref, (i * sub + s) % subs_per_seq == 0)
        gate, up = c[:, :tn], c[:, tn:]
        o_ref[rows, :] = (gate * jax.nn.sigmoid(gate) * up).astype(o_ref.dtype)


def _ffn_up(x, w_up, conv_w, layer, *, d_ff, seq, tm, tn, sub):
    m, k = x.shape
    nt = d_ff // tn
    kern = functools.partial(_ffn_up_kernel, sub=sub, subs_per_seq=seq * sub // tm)
    return pl.pallas_call(
        kern,
        grid=(nt, m // tm),
        in_specs=[pl.BlockSpec((tm, k), lambda j, i: (i, 0)),
                  pl.BlockSpec((None, k, tn), lambda j, i: (layer, 0, j)),
                  pl.BlockSpec((None, k, tn), lambda j, i: (layer, 0, j + nt)),
                  pl.BlockSpec((None, 3, tn), lambda j, i: (layer, 0, j)),
                  pl.BlockSpec((None, 3, tn), lambda j, i: (layer, 0, j + nt))],
        out_specs=pl.BlockSpec((tm, tn), lambda j, i: (i, j)),
        out_shape=jax.ShapeDtypeStruct((m, d_ff), BF16),
        scratch_shapes=[pltpu.VMEM((k, 2 * tn), BF16), pltpu.VMEM((sub, CONV_HALO + tm // sub, 2 * tn), F32),
                        pltpu.VMEM((CONV_HALO, 2 * tn), F32)],
        compiler_params=_params("arbitrary", "arbitrary"),
        name="ffn_up",
    )(x, w_up, w_up, conv_w, conv_w)


def _out_proj_kernel(yc_ref, yd_ref, ys_ref, w_ref, h_ref, o_ref, *, alpha):
    r1 = yc_ref.shape[1]
    r2 = r1 + yd_ref.shape[1]
    mix = (_dot(yc_ref[...], w_ref[0:r1, :]) + _dot(yd_ref[...], w_ref[r1:r2, :])
           + _dot(ys_ref[...], w_ref[r2:, :]))
    o_ref[...] = alpha * h_ref[...] + mix


def _out_proj(yc, yd, ys, w, h, layer, *, alpha, tm, tn):
    m, d = h.shape
    lhs = lambda a: pl.BlockSpec((tm, a.shape[1]), lambda i, j: (i, 0))
    return pl.pallas_call(
        functools.partial(_out_proj_kernel, alpha=alpha),
        grid=(m // tm, d // tn),
        in_specs=[lhs(yc), lhs(yd), lhs(ys),
                  pl.BlockSpec((None, w.shape[1], tn), lambda i, j: (layer, 0, j)),
                  pl.BlockSpec((tm, tn), lambda i, j: (i, j))],
        out_specs=pl.BlockSpec((tm, tn), lambda i, j: (i, j)),
        out_shape=jax.ShapeDtypeStruct((m, d), F32),
        compiler_params=_params("parallel", "arbitrary"),
        name="out_proj",
    )(yc, yd, ys, w, h)


def _mm_res_kernel(x_ref, w_ref, h_ref, o_ref, *, alpha):
    o_ref[...] = alpha * h_ref[...] + _dot(x_ref[...], w_ref[...])


def _ffn_down(a, w, h, layer, *, alpha, tm, tn):
    m, d = h.shape
    k = a.shape[1]
    return pl.pallas_call(
        functools.partial(_mm_res_kernel, alpha=alpha),
        grid=(m // tm, d // tn),
        in_specs=[pl.BlockSpec((tm, k), lambda i, j: (i, 0)),
                  pl.BlockSpec((None, k, tn), lambda i, j: (layer, 0, j)),
                  pl.BlockSpec((tm, tn), lambda i, j: (i, j))],
        out_specs=pl.BlockSpec((tm, tn), lambda i, j: (i, j)),
        out_shape=jax.ShapeDtypeStruct((m, d), F32),
        compiler_params=_params("parallel", "arbitrary"),
        name="ffn_down",
    )(a, w, h)


def _pad_rows(a, rows):
    if a.shape[0] == rows:
        return a
    return jnp.concatenate([a, jnp.zeros((rows - a.shape[0], a.shape[1]), a.dtype)], axis=0)


def _head_cols(h):
    return slice(h * HEAD_DIM, (h + 1) * HEAD_DIM)


def _num_key_blocks(seq):
    return -(-seq // ATT_BLOCK)


def _stage_values_transposed(v_ref, vt_ref, seq, nh):
    for h in range(nh):
        for j in range(_num_key_blocks(seq)):
            r0 = j * ATT_BLOCK
            v = _pad_rows(v_ref[r0:min(r0 + ATT_BLOCK, seq), _head_cols(h)], ATT_BLOCK)
            vt_ref[h, j, 0:HEAD_DIM, :] = v.astype(F32).T.astype(BF16)


def _row_block(ref, j, t, nh):
    r0 = j * ATT_BLOCK
    if not isinstance(j, int):
        r0 = pl.multiple_of(r0, ATT_BLOCK)
    return [_pad_rows(ref[pl.ds(r0, t), _head_cols(h)], ATT_BLOCK) for h in range(nh)]


def _causal_sweep(seq, q_block):
    n_full, tail = seq // ATT_BLOCK, seq % ATT_BLOCK

    def full_block(i, _):
        q_block(i, ATT_BLOCK, ATT_BLOCK)
        return 0

    lax.fori_loop(0, n_full, full_block, 0)
    if tail:
        q_block(n_full, tail, ATT_TAIL_LANES)


def _key_query_iota(tp, maps):
    r = lax.broadcasted_iota(jnp.int32, (ATT_BLOCK, maps * tp), 0)
    c = lax.broadcasted_iota(jnp.int32, (ATT_BLOCK, maps * tp), 1)
    return r, (jnp.where(c >= tp, c - tp, c) if maps == 2 else c)


def _diff_attn_kernel(lam_ref, q_ref, k_ref, v_ref, g_ref, o_ref, vt_ref, acc_ref, *, seq, gain, nh):
    blk = ATT_BLOCK
    heads = range(nh)
    lam = lam_ref[0]
    g = g_ref[...]
    lo_lane = lax.broadcasted_iota(jnp.int32, (1, HEAD_DIM), 1) < DIFF_QK_DIM
    _stage_values_transposed(v_ref, vt_ref, seq, nh)
    for h in heads:
        for j in range(_num_key_blocks(seq)):
            vt_ref[h, j, HEAD_DIM:, :] = jnp.ones((ONES_ROWS, blk), BF16)

    def stacked_q(h, q0, t, tp):
        q = (q_ref[pl.ds(q0, t), _head_cols(h)].astype(F32) * (DIFF_QK_DIM ** -0.5)).astype(BF16)
        q = _pad_rows(q, tp)
        zero = jnp.zeros_like(q)
        return jnp.concatenate([jnp.where(lo_lane, q, zero), jnp.where(lo_lane, zero, q)], axis=0)

    def block_step(qs, ks, j, mask, ms):
        w = qs[0].shape[0]
        ss = [_dot_nt(ks[h], qs[h]) for h in heads]
        if mask is not None:
            ss = [jnp.where(mask, s, -jnp.inf) for s in ss]
        m_new = [jnp.maximum(ms[h], jnp.max(ss[h], axis=0, keepdims=True)) for h in heads]
        ps = [jnp.exp(ss[h] - m_new[h]).astype(BF16) for h in heads]
        pvs = [_dot(vt_ref[h, j], ps[h]) for h in heads]
        for h in heads:
            acc_ref[h, :, 0:w] = jnp.exp(ms[h] - m_new[h]) * acc_ref[h, :, 0:w] + pvs[h]
        return m_new

    def q_block(i, t, tp):
        q0 = i * blk if isinstance(i, int) else pl.multiple_of(i * blk, blk)
        qs = [stacked_q(h, q0, t, tp) for h in heads]
        for h in heads:
            acc_ref[h, :, 0:2 * tp] = jnp.zeros((HEAD_DIM + ONES_ROWS, 2 * tp), F32)

        def kv_step(j, ms):
            return tuple(block_step(qs, _row_block(k_ref, j, blk, nh), j, None, ms))

        ms = lax.fori_loop(0, i, kv_step, (jnp.full((1, 2 * tp), -jnp.inf, F32),) * nh)
        r, c = _key_query_iota(tp, 2)
        block_step(qs, _row_block(k_ref, i, t, nh), i, r <= c, ms)
        for h in heads:
            acc = acc_ref[h, :, 0:2 * tp]
            o = acc[:HEAD_DIM] * (1.0 / acc[HEAD_DIM:HEAD_DIM + 1])
            o = o[:, :tp] - lam * o[:, tp:]
            o = o * lax.rsqrt(jnp.mean(o * o, axis=0, keepdims=True) + LN_EPS)
            o = (o * g * gain).T
            o_ref[pl.ds(q0, t), _head_cols(h)] = o[:t].astype(o_ref.dtype)

    _causal_sweep(seq, q_block)


def _diff_attention(proj, lam, gain_g, *, heads, col_off, lam_init, nh):
    b, seq, _ = proj.shape
    w = nh * HEAD_DIM
    c0 = col_off // w
    spec = lambda s: pl.BlockSpec((None, seq, w), lambda i, h: (i, 0, c0 + s * (heads // nh) + h))
    return pl.pallas_call(
        functools.partial(_diff_attn_kernel, seq=seq, gain=1.0 - lam_init, nh=nh),
        grid=(b, heads // nh),
        in_specs=[pl.BlockSpec(memory_space=pltpu.SMEM), spec(0), spec(1), spec(2),
                  pl.BlockSpec((HEAD_DIM, 1), lambda i, h: (0, 0))],
        out_specs=pl.BlockSpec((None, seq, w), lambda i, h: (i, 0, h)),
        out_shape=jax.ShapeDtypeStruct((b, seq, heads * HEAD_DIM), BF16),
        scratch_shapes=[pltpu.VMEM((nh, _num_key_blocks(seq), HEAD_DIM + ONES_ROWS, ATT_BLOCK), BF16),
                        pltpu.VMEM((nh, HEAD_DIM + ONES_ROWS, 2 * ATT_BLOCK), F32)],
        compiler_params=_params("parallel", "parallel"),
        name="diff_attn",
    )(lam, proj, proj, proj, gain_g.reshape(HEAD_DIM, 1))


def _sb_attn_kernel(q_ref, k_ref, v_ref, u_ref, o_ref, acc_ref, *, seq, nh):
    blk = ATT_BLOCK
    heads = range(nh)
    scale = HEAD_DIM ** -0.5

    def block_step(qs, ks, vs, mask, runs):
        t = qs[0].shape[0]
        zs = [_dot_nt(qs[h], ks[h]) * scale for h in heads]
        sps = [jnp.maximum(z, 0.0) + jnp.log(1.0 + jnp.exp(-jnp.abs(z))) for z in zs]
        log_keeps = [-sp if mask is None else jnp.where(mask, -sp, 0.0) for sp in sps]
        his = [lk.astype(BF16) for lk in log_keeps]
        los = [(lk - hi.astype(F32)).astype(BF16) for lk, hi in zip(log_keeps, his)]
        incls = [_dot(jnp.concatenate([his[h], los[h]], axis=1), u_ref[...]) for h in heads]
        ws = [jnp.exp(zs[h] + incls[h] + runs[h]) for h in heads]
        if mask is not None:
            ws = [jnp.where(mask, w, 0.0) for w in ws]
        pvs = [_dot(ws[h].astype(BF16), vs[h]) for h in heads]
        for h in heads:
            acc_ref[h, 0:t, :] = pvs[h] if mask is not None else acc_ref[h, 0:t, :] + pvs[h]
        return [runs[h] + jnp.sum(log_keeps[h], axis=-1, keepdims=True) for h in heads]

    def q_block(i, t, tp):
        del tp
        q0 = i * blk if isinstance(i, int) else pl.multiple_of(i * blk, blk)
        qs = [q_ref[pl.ds(q0, t), _head_cols(h)] for h in heads]
        r = lax.broadcasted_iota(jnp.int32, (t, blk), 0)
        c = lax.broadcasted_iota(jnp.int32, (t, blk), 1)
        runs = block_step(qs, _row_block(k_ref, i, t, nh), _row_block(v_ref, i, t, nh), c < r,
                          [jnp.zeros((t, 1), F32)] * nh)

        def any_live(runs):
            return jnp.max(functools.reduce(jnp.maximum, runs)) > SB_DEAD_TAIL

        def kv_cond(state):
            return (state[0] < i) & state[1]

        def kv_step(state):
            jj = state[0]
            j = i - 1 - jj
            runs = block_step(qs, _row_block(k_ref, j, blk, nh), _row_block(v_ref, j, blk, nh), None, state[2:])
            return (jj + 1, any_live(runs)) + tuple(runs)

        lax.while_loop(kv_cond, kv_step, (jnp.int32(0), any_live(runs)) + tuple(runs))
        for h in heads:
            o_ref[pl.ds(q0, t), _head_cols(h)] = acc_ref[h, 0:t, :].astype(o_ref.dtype)

    _causal_sweep(seq, q_block)


def _sb_attention(proj, *, heads, col_off, nh):
    b, seq, _ = proj.shape
    w = nh * HEAD_DIM
    c0 = col_off // w
    blk = ATT_BLOCK
    tri = (lax.broadcasted_iota(jnp.int32, (blk, blk), 0) >= lax.broadcasted_iota(jnp.int32, (blk, blk), 1))
    u2 = jnp.concatenate([tri, tri], axis=0).astype(BF16)
    spec = lambda s: pl.BlockSpec((None, seq, w), lambda i, h: (i, 0, c0 + s * (heads // nh) + h))
    return pl.pallas_call(
        functools.partial(_sb_attn_kernel, seq=seq, nh=nh),
        grid=(b, heads // nh),
        in_specs=[spec(0), spec(1), spec(2), pl.BlockSpec((2 * blk, blk), lambda i, h: (0, 0))],
        out_specs=pl.BlockSpec((None, seq, w), lambda i, h: (i, 0, h)),
        out_shape=jax.ShapeDtypeStruct((b, seq, heads * HEAD_DIM), BF16),
        scratch_shapes=[pltpu.VMEM((nh, ATT_BLOCK, HEAD_DIM), F32)],
        compiler_params=_params("parallel", "parallel"),
        name="sb_attn",
    )(proj, proj, proj, u2)


def kernel(x, meta_tokens, emb_ln_g, emb_ln_b, w_in, short_conv_w, lambda_q1, lambda_k1, lambda_q2, lambda_k2,
           diff_norm_g, w_out, ln1_g, ln1_b, w_up, ffn_conv_w, w_down, ln2_g, ln2_b):
    bsz, _, d = x.shape
    depth = w_in.shape[0]
    conv_w = short_conv_w.shape[-1]
    d_ff = w_down.shape[1]
    diff_w = (w_in.shape[-1] - 3 * conv_w) // 6
    heads = diff_w // HEAD_DIM
    alpha = (2 * depth) ** 0.25

    meta = jnp.broadcast_to(meta_tokens[None].astype(x.dtype), (bsz, N_META, d))
    tokens = jnp.concatenate([meta, x], axis=1)
    seq = tokens.shape[1]
    rows = bsz * seq
    tm_seq = seq // 3
    tm_big = 2 * tm_seq
    assert seq % 3 == 0 and tm_seq % 16 == 0 and rows % tm_big == 0, "row tiling assumes seq = 3 * 16k rows"
    tr = _pick_tile(rows, (192, 176, 96, 48, 16))
    tn_wide = _pick_tile(math.gcd(6 * diff_w, d), (512, 256, 128))
    tn_conv = _pick_tile(math.gcd(conv_w, d_ff), (256, 128))
    nh = _pick_tile(heads, (ATT_HEADS_PER_STEP, 1))

    w_down_b = w_down.astype(BF16)
    w_out_b = w_out.astype(BF16)

    h, hb = _layer_norm(tokens.reshape(rows, d), emb_ln_g, emb_ln_b, tr=tr)
    for l in range(depth):
        lam_init = _lambda_init(l)
        lam = (jnp.exp(jnp.sum(lambda_q1[l] * lambda_k1[l])) - jnp.exp(jnp.sum(lambda_q2[l] * lambda_k2[l]))
               + lam_init).reshape(1).astype(F32)
        y_conv = _conv_mixer(hb, w_in, short_conv_w, l, width=conv_w, seq=seq, tm=tm_seq, tn=tn_conv)
        proj = _project(hb, w_in, l, col_off=3 * conv_w, n_cols=6 * diff_w, tm=tm_big, tn=tn_wide, out_dtype=BF16)
        proj = proj.reshape(bsz, seq, 6 * diff_w)
        y_diff = _diff_attention(proj, lam, diff_norm_g[l], heads=heads, col_off=0, lam_init=lam_init, nh=nh)
        y_sb = _sb_attention(proj, heads=heads, col_off=3 * diff_w, nh=nh)
        x1 = _out_proj(y_conv, y_diff.reshape(rows, diff_w), y_sb.reshape(rows, diff_w),
                       w_out_b, h, l, alpha=alpha, tm=tm_big, tn=tn_wide)
        h, hb = _layer_norm(x1, ln1_g[l], ln1_b[l], tr=tr)
        act = _ffn_up(hb, w_up, ffn_conv_w, l, d_ff=d_ff, seq=seq, tm=tm_big, tn=tn_conv, sub=2)
        x2 = _ffn_down(act, w_down_b, h, l, alpha=alpha, tm=tm_seq, tn=tn_wide)
        if l + 1 < depth:
            h, hb = _layer_norm(x2, ln2_g[l], ln2_b[l], tr=tr)
    return _layer_norm_drop_meta(x2, ln2_g[depth - 1], ln2_b[depth - 1], bsz=bsz, seq=seq,
                                 tr=_pick_tile(seq - N_META, (256, 128, 64, 16)))
```

```python
import functools
import math

import jax
import jax.numpy as jnp
from jax import lax
from jax.experimental import pallas as pl
from jax.experimental.pallas import tpu as pltpu

N_META = 16
HEAD_DIM = 128
DIFF_QK_DIM = HEAD_DIM // 2
LN_EPS = 1e-5
ATT_BLOCK = 256
ATT_HEADS_PER_STEP = 6
ATT_SCORE_LOOKAHEAD = 1
ATT_TAIL_LANES = 64
ONES_ROWS = 16
SB_DEAD_TAIL = -100.0
CONV_HALO = 8
VMEM_LIMIT = 60 * 1024 * 1024

F32 = jnp.float32
BF16 = jnp.bfloat16


def _lambda_init(layer):
    return 0.8 - 0.6 * math.exp(-0.3 * layer)


def _pick_tile(n, candidates):
    return next(c for c in candidates if n % c == 0)


def _params(*sem):
    return pltpu.CompilerParams(dimension_semantics=sem, vmem_limit_bytes=VMEM_LIMIT)


def _dot(a, b):
    return jnp.dot(a, b, preferred_element_type=F32)


def _dot_nt(a, b):
    return lax.dot_general(a, b, (((1,), (1,)), ((), ())), preferred_element_type=F32)


def _ln_kernel(x_ref, g_ref, b_ref, of_ref, ob_ref):
    x = x_ref[...]
    mu = jnp.mean(x, axis=-1, keepdims=True)
    xc = x - mu
    var = jnp.mean(xc * xc, axis=-1, keepdims=True)
    y = xc * lax.rsqrt(var + LN_EPS) * g_ref[...] + b_ref[...]
    of_ref[...] = y
    ob_ref[...] = y.astype(BF16)


def _layer_norm(x, g, b, *, tr):
    rows, d = x.shape
    return pl.pallas_call(
        _ln_kernel,
        grid=(rows // tr,),
        in_specs=[pl.BlockSpec((tr, d), lambda i: (i, 0)),
                  pl.BlockSpec((1, d), lambda i: (0, 0)),
                  pl.BlockSpec((1, d), lambda i: (0, 0))],
        out_specs=[pl.BlockSpec((tr, d), lambda i: (i, 0)),
                   pl.BlockSpec((tr, d), lambda i: (i, 0))],
        out_shape=[jax.ShapeDtypeStruct((rows, d), F32), jax.ShapeDtypeStruct((rows, d), BF16)],
        compiler_params=_params("parallel"),
        name="layer_norm",
    )(x, g.reshape(1, d), b.reshape(1, d))


def _ln_final_kernel(x_ref, g_ref, b_ref, o_ref):
    x = x_ref[...]
    mu = jnp.mean(x, axis=-1, keepdims=True)
    xc = x - mu
    var = jnp.mean(xc * xc, axis=-1, keepdims=True)
    o_ref[...] = xc * lax.rsqrt(var + LN_EPS) * g_ref[...] + b_ref[...]


def _layer_norm_drop_meta(x, g, b, *, bsz, seq, tr):
    d = x.shape[1]
    real = seq - N_META
    steps = real // tr
    out = pl.pallas_call(
        _ln_final_kernel,
        grid=(bsz, steps),
        in_specs=[pl.BlockSpec((pl.Element(tr), pl.Element(d)), lambda i, j: (8 * (i * (seq // 8) + N_META // 8 + j * (tr // 8)), 0)),
                  pl.BlockSpec((1, d), lambda i, j: (0, 0)),
                  pl.BlockSpec((1, d), lambda i, j: (0, 0))],
        out_specs=pl.BlockSpec((tr, d), lambda i, j: (i * steps + j, 0)),
        out_shape=jax.ShapeDtypeStruct((bsz * real, d), F32),
        compiler_params=_params("parallel", "parallel"),
        name="layer_norm_out",
    )(x, g.reshape(1, d), b.reshape(1, d))
    return out.reshape(bsz, real, d)


def _cast_weights_once(w_refs, w_scr):
    @pl.when(pl.program_id(1) == 0)
    def _():
        col = 0
        for w_ref in w_refs:
            w_scr[:, col:col + w_ref.shape[1]] = w_ref[...].astype(BF16)
            col += w_ref.shape[1]


def _mm_kernel(x_ref, w_ref, o_ref, w_scr):
    _cast_weights_once([w_ref], w_scr)
    o_ref[...] = _dot(x_ref[...], w_scr[...]).astype(o_ref.dtype)


def _project(x, w, layer, *, col_off, n_cols, tm, tn, out_dtype):
    m, k = x.shape
    off = col_off // tn
    return pl.pallas_call(
        _mm_kernel,
        grid=(n_cols // tn, m // tm),
        in_specs=[pl.BlockSpec((tm, k), lambda j, i: (i, 0)),
                  pl.BlockSpec((None, k, tn), lambda j, i: (layer, 0, j + off))],
        out_specs=pl.BlockSpec((tm, tn), lambda j, i: (i, j)),
        out_shape=jax.ShapeDtypeStruct((m, n_cols), out_dtype),
        scratch_shapes=[pltpu.VMEM((k, tn), BF16)],
        compiler_params=_params("arbitrary", "arbitrary"),
        name="proj_attn",
    )(x, w)


def _causal_conv3(u, w, scr_ref, halo_ref, first_tile):
    tm = u.shape[0]
    scr_ref[0:CONV_HALO, :] = jnp.where(first_tile, 0.0, halo_ref[...])
    scr_ref[CONV_HALO:CONV_HALO + tm, :] = u
    halo_ref[...] = u[tm - CONV_HALO:tm, :]
    return (w[0:1, :] * scr_ref[CONV_HALO - 2:CONV_HALO - 2 + tm, :]
            + w[1:2, :] * scr_ref[CONV_HALO - 1:CONV_HALO - 1 + tm, :]
            + w[2:3, :] * u)


def _conv_mixer_kernel(x_ref, wb_ref, wc_ref, wh_ref, cw_ref, o_ref, w_scr, scr_ref, halo_ref, *, tiles_per_seq):
    i = pl.program_id(1)
    tn = o_ref.shape[1]
    _cast_weights_once([wb_ref, wc_ref, wh_ref], w_scr)

    @pl.when(i == 0)
    def _():
        halo_ref[...] = jnp.zeros(halo_ref.shape, F32)

    u = _dot(x_ref[...], w_scr[...])
    g = u[:, tn:2 * tn] * u[:, 2 * tn:]
    y = _causal_conv3(g, cw_ref[...], scr_ref, halo_ref, i % tiles_per_seq == 0)
    o_ref[...] = (u[:, :tn] * y).astype(o_ref.dtype)


def _conv_mixer(x, w_in, conv_w, layer, *, width, seq, tm, tn):
    m, k = x.shape
    nt = width // tn
    kern = functools.partial(_conv_mixer_kernel, tiles_per_seq=seq // tm)
    wspec = lambda s: pl.BlockSpec((None, k, tn), lambda j, i: (layer, 0, j + s * nt))
    return pl.pallas_call(
        kern,
        grid=(nt, m // tm),
        in_specs=[pl.BlockSpec((tm, k), lambda j, i: (i, 0)), wspec(0), wspec(1), wspec(2),
                  pl.BlockSpec((None, 3, tn), lambda j, i: (layer, 0, j))],
        out_specs=pl.BlockSpec((tm, tn), lambda j, i: (i, j)),
        out_shape=jax.ShapeDtypeStruct((m, width), BF16),
        scratch_shapes=[pltpu.VMEM((k, 3 * tn), BF16), pltpu.VMEM((CONV_HALO + tm, tn), F32),
                        pltpu.VMEM((CONV_HALO, tn), F32)],
        compiler_params=_params("arbitrary", "arbitrary"),
        name="conv_mixer",
    )(x, w_in, w_in, w_in, conv_w)


def _ffn_up_kernel(x_ref, wg_ref, wu_ref, cg_ref, cu_ref, o_ref, w_scr, scr_ref, halo_ref, *, sub, subs_per_seq):
    i = pl.program_id(1)
    tn = o_ref.shape[1]
    ts = o_ref.shape[0] // sub
    _cast_weights_once([wg_ref, wu_ref], w_scr)

    @pl.when(i == 0)
    def _():
        halo_ref[...] = jnp.zeros(halo_ref.shape, F32)

    cw = jnp.concatenate([cg_ref[...], cu_ref[...]], axis=1)
    for s in range(sub):
        rows = slice(s * ts, (s + 1) * ts)
        u = _dot(x_ref[rows, :], w_scr[...])
        c = _causal_conv3(u, cw, scr_ref.at[s], halo_ref, (i * sub + s) % subs_per_seq == 0)
        gate, up = c[:, :tn], c[:, tn:]
        o_ref[rows, :] = (gate * jax.nn.sigmoid(gate) * up).astype(o_ref.dtype)


def _ffn_up(x, w_up, conv_w, layer, *, d_ff, seq, tm, tn, sub):
    m, k = x.shape
    nt = d_ff // tn
    kern = functools.partial(_ffn_up_kernel, sub=sub, subs_per_seq=seq * sub // tm)
    return pl.pallas_call(
        kern,
        grid=(nt, m // tm),
        in_specs=[pl.BlockSpec((tm, k), lambda j, i: (i, 0)),
                  pl.BlockSpec((None, k, tn), lambda j, i: (layer, 0, j)),
                  pl.BlockSpec((None, k, tn), lambda j, i: (layer, 0, j + nt)),
                  pl.BlockSpec((None, 3, tn), lambda j, i: (layer, 0, j)),
                  pl.BlockSpec((None, 3, tn), lambda j, i: (layer, 0, j + nt))],
        out_specs=pl.BlockSpec((tm, tn), lambda j, i: (i, j)),
        out_shape=jax.ShapeDtypeStruct((m, d_ff), BF16),
        scratch_shapes=[pltpu.VMEM((k, 2 * tn), BF16), pltpu.VMEM((sub, CONV_HALO + tm // sub, 2 * tn), F32),
                        pltpu.VMEM((CONV_HALO, 2 * tn), F32)],
        compiler_params=_params("arbitrary", "arbitrary"),
        name="ffn_up",
    )(x, w_up, w_up, conv_w, conv_w)


def _out_proj_kernel(yc_ref, yd_ref, ys_ref, w_ref, h_ref, o_ref, *, alpha):
    r1 = yc_ref.shape[1]
    r2 = r1 + yd_ref.shape[1]
    mix = (_dot(yc_ref[...], w_ref[0:r1, :]) + _dot(yd_ref[...], w_ref[r1:r2, :])
           + _dot(ys_ref[...], w_ref[r2:, :]))
    o_ref[...] = alpha * h_ref[...] + mix


def _out_proj(yc, yd, ys, w, h, layer, *, alpha, tm, tn):
    m, d = h.shape
    lhs = lambda a: pl.BlockSpec((tm, a.shape[1]), lambda i, j: (i, 0))
    return pl.pallas_call(
        functools.partial(_out_proj_kernel, alpha=alpha),
        grid=(m // tm, d // tn),
        in_specs=[lhs(yc), lhs(yd), lhs(ys),
                  pl.BlockSpec((None, w.shape[1], tn), lambda i, j: (layer, 0, j)),
                  pl.BlockSpec((tm, tn), lambda i, j: (i, j))],
        out_specs=pl.BlockSpec((tm, tn), lambda i, j: (i, j)),
        out_shape=jax.ShapeDtypeStruct((m, d), F32),
        compiler_params=_params("parallel", "arbitrary"),
        name="out_proj",
    )(yc, yd, ys, w, h)


def _mm_res_kernel(x_ref, w_ref, h_ref, o_ref, *, alpha):
    o_ref[...] = alpha * h_ref[...] + _dot(x_ref[...], w_ref[...])


def _ffn_down(a, w, h, layer, *, alpha, tm, tn):
    m, d = h.shape
    k = a.shape[1]
    return pl.pallas_call(
        functools.partial(_mm_res_kernel, alpha=alpha),
        grid=(m // tm, d // tn),
        in_specs=[pl.BlockSpec((tm, k), lambda i, j: (i, 0)),
                  pl.BlockSpec((None, k, tn), lambda i, j: (layer, 0, j)),
                  pl.BlockSpec((tm, tn), lambda i, j: (i, j))],
        out_specs=pl.BlockSpec((tm, tn), lambda i, j: (i, j)),
        out_shape=jax.ShapeDtypeStruct((m, d), F32),
        compiler_params=_params("parallel", "arbitrary"),
        name="ffn_down",
    )(a, w, h)


def _pad_rows(a, rows):
    if a.shape[0] == rows:
        return a
    return jnp.concatenate([a, jnp.zeros((rows - a.shape[0], a.shape[1]), a.dtype)], axis=0)


def _head_cols(h):
    return slice(h * HEAD_DIM, (h + 1) * HEAD_DIM)


def _num_key_blocks(seq):
    return -(-seq // ATT_BLOCK)


def _stage_values_transposed(v_ref, vt_ref, seq, nh):
    for h in range(nh):
        for j in range(_num_key_blocks(seq)):
            r0 = j * ATT_BLOCK
            v = _pad_rows(v_ref[r0:min(r0 + ATT_BLOCK, seq), _head_cols(h)], ATT_BLOCK)
            vt_ref[h, j, 0:HEAD_DIM, :] = v.astype(F32).T.astype(BF16)


def _row_block(ref, j, t, nh):
    r0 = j * ATT_BLOCK
    if not isinstance(j, int):
        r0 = pl.multiple_of(r0, ATT_BLOCK)
    return [_pad_rows(ref[pl.ds(r0, t), _head_cols(h)], ATT_BLOCK) for h in range(nh)]


def _causal_sweep(seq, q_block):
    n_full, tail = seq // ATT_BLOCK, seq % ATT_BLOCK

    def full_block(i, _):
        q_block(i, ATT_BLOCK, ATT_BLOCK)
        return 0

    lax.fori_loop(0, n_full, full_block, 0)
    if tail:
        q_block(n_full, tail, ATT_TAIL_LANES)


def _key_query_iota(tp, maps):
    r = lax.broadcasted_iota(jnp.int32, (ATT_BLOCK, maps * tp), 0)
    c = lax.broadcasted_iota(jnp.int32, (ATT_BLOCK, maps * tp), 1)
    return r, (jnp.where(c >= tp, c - tp, c) if maps == 2 else c)


def _diff_attn_kernel(lam_ref, q_ref, k_ref, v_ref, g_ref, o_ref, vt_ref, acc_ref, *, seq, gain, nh):
    blk = ATT_BLOCK
    heads = range(nh)
    lam = lam_ref[0]
    g = g_ref[...]
    lo_lane = lax.broadcasted_iota(jnp.int32, (1, HEAD_DIM), 1) < DIFF_QK_DIM
    _stage_values_transposed(v_ref, vt_ref, seq, nh)
    for h in heads:
        for j in range(_num_key_blocks(seq)):
            vt_ref[h, j, HEAD_DIM:, :] = jnp.ones((ONES_ROWS, blk), BF16)

    def stacked_q(h, q0, t, tp):
        q = (q_ref[pl.ds(q0, t), _head_cols(h)].astype(F32) * (DIFF_QK_DIM ** -0.5)).astype(BF16)
        q = _pad_rows(q, tp)
        zero = jnp.zeros_like(q)
        return jnp.concatenate([jnp.where(lo_lane, q, zero), jnp.where(lo_lane, zero, q)], axis=0)

    def block_step(qs, ks, j, mask, ms):
        w = qs[0].shape[0]

        def scores(h):
            s = _dot_nt(ks[h], qs[h])
            return s if mask is None else jnp.where(mask, s, -jnp.inf)

        m_new = []
        pending = [scores(h) for h in range(min(ATT_SCORE_LOOKAHEAD, nh))]
        for h in heads:
            s = pending.pop(0)
            if h + ATT_SCORE_LOOKAHEAD < nh:
                pending.append(scores(h + ATT_SCORE_LOOKAHEAD))
            m_h = jnp.maximum(ms[h], jnp.max(s, axis=0, keepdims=True))
            pv = _dot(vt_ref[h, j], jnp.exp(s - m_h).astype(BF16))
            acc_ref[h, :, 0:w] = jnp.exp(ms[h] - m_h) * acc_ref[h, :, 0:w] + pv
            m_new.append(m_h)
        return m_new

    def q_block(i, t, tp):
        q0 = i * blk if isinstance(i, int) else pl.multiple_of(i * blk, blk)
        qs = [stacked_q(h, q0, t, tp) for h in heads]
        for h in heads:
            acc_ref[h, :, 0:2 * tp] = jnp.zeros((HEAD_DIM + ONES_ROWS, 2 * tp), F32)

        def kv_step(j, ms):
            return tuple(block_step(qs, _row_block(k_ref, j, blk, nh), j, None, ms))

        ms = lax.fori_loop(0, i, kv_step, (jnp.full((1, 2 * tp), -jnp.inf, F32),) * nh)
        r, c = _key_query_iota(tp, 2)
        block_step(qs, _row_block(k_ref, i, t, nh), i, r <= c, ms)
        for h in heads:
            acc = acc_ref[h, :, 0:2 * tp]
            o = acc[:HEAD_DIM] * (1.0 / acc[HEAD_DIM:HEAD_DIM + 1])
            o = o[:, :tp] - lam * o[:, tp:]
            o = o * lax.rsqrt(jnp.mean(o * o, axis=0, keepdims=True) + LN_EPS)
            o = (o * g * gain).T
            o_ref[pl.ds(q0, t), _head_cols(h)] = o[:t].astype(o_ref.dtype)

    _causal_sweep(seq, q_block)


def _diff_attention(proj, lam, gain_g, *, heads, col_off, lam_init, nh):
    b, seq, _ = proj.shape
    w = nh * HEAD_DIM
    c0 = col_off // w
    spec = lambda s: pl.BlockSpec((None, seq, w), lambda i, h: (i, 0, c0 + s * (heads // nh) + h))
    return pl.pallas_call(
        functools.partial(_diff_attn_kernel, seq=seq, gain=1.0 - lam_init, nh=nh),
        grid=(b, heads // nh),
        in_specs=[pl.BlockSpec(memory_space=pltpu.SMEM), spec(0), spec(1), spec(2),
                  pl.BlockSpec((HEAD_DIM, 1), lambda i, h: (0, 0))],
        out_specs=pl.BlockSpec((None, seq, w), lambda i, h: (i, 0, h)),
        out_shape=jax.ShapeDtypeStruct((b, seq, heads * HEAD_DIM), BF16),
        scratch_shapes=[pltpu.VMEM((nh, _num_key_blocks(seq), HEAD_DIM + ONES_ROWS, ATT_BLOCK), BF16),
                        pltpu.VMEM((nh, HEAD_DIM + ONES_ROWS, 2 * ATT_BLOCK), F32)],
        compiler_params=_params("parallel", "parallel"),
        name="diff_attn",
    )(lam, proj, proj, proj, gain_g.reshape(HEAD_DIM, 1))


def _sb_attn_kernel(q_ref, k_ref, v_ref, u_ref, o_ref, acc_ref, *, seq, nh):
    blk = ATT_BLOCK
    heads = range(nh)
    scale = HEAD_DIM ** -0.5

    def block_step(qs, ks, vs, mask, runs):
        t = qs[0].shape[0]
        zs = [_dot_nt(qs[h], ks[h]) * scale for h in heads]
        sps = [jnp.maximum(z, 0.0) + jnp.log(1.0 + jnp.exp(-jnp.abs(z))) for z in zs]
        log_keeps = [-sp if mask is None else jnp.where(mask, -sp, 0.0) for sp in sps]
        his = [lk.astype(BF16) for lk in log_keeps]
        los = [(lk - hi.astype(F32)).astype(BF16) for lk, hi in zip(log_keeps, his)]
        incls = [_dot(jnp.concatenate([his[h], los[h]], axis=1), u_ref[...]) for h in heads]
        ws = [jnp.exp(zs[h] + incls[h] + runs[h]) for h in heads]
        if mask is not None:
            ws = [jnp.where(mask, w, 0.0) for w in ws]
        pvs = [_dot(ws[h].astype(BF16), vs[h]) for h in heads]
        for h in heads:
            acc_ref[h, 0:t, :] = pvs[h] if mask is not None else acc_ref[h, 0:t, :] + pvs[h]
        return [runs[h] + jnp.sum(log_keeps[h], axis=-1, keepdims=True) for h in heads]

    def q_block(i, t, tp):
        del tp
        q0 = i * blk if isinstance(i, int) else pl.multiple_of(i * blk, blk)
        qs = [q_ref[pl.ds(q0, t), _head_cols(h)] for h in heads]
        r = lax.broadcasted_iota(jnp.int32, (t, blk), 0)
        c = lax.broadcasted_iota(jnp.int32, (t, blk), 1)
        runs = block_step(qs, _row_block(k_ref, i, t, nh), _row_block(v_ref, i, t, nh), c < r,
                          [jnp.zeros((t, 1), F32)] * nh)

        def any_live(runs):
            return jnp.max(functools.reduce(jnp.maximum, runs)) > SB_DEAD_TAIL

        def kv_cond(state):
            return (state[0] < i) & state[1]

        def kv_step(state):
            jj = state[0]
            j = i - 1 - jj
            runs = block_step(qs, _row_block(k_ref, j, blk, nh), _row_block(v_ref, j, blk, nh), None, state[2:])
            return (jj + 1, any_live(runs)) + tuple(runs)

        lax.while_loop(kv_cond, kv_step, (jnp.int32(0), any_live(runs)) + tuple(runs))
        for h in heads:
            o_ref[pl.ds(q0, t), _head_cols(h)] = acc_ref[h, 0:t, :].astype(o_ref.dtype)

    _causal_sweep(seq, q_block)


def _sb_attention(proj, *, heads, col_off, nh):
    b, seq, _ = proj.shape
    w = nh * HEAD_DIM
    c0 = col_off // w
    blk = ATT_BLOCK
    tri = (lax.broadcasted_iota(jnp.int32, (blk, blk), 0) >= lax.broadcasted_iota(jnp.int32, (blk, blk), 1))
    u2 = jnp.concatenate([tri, tri], axis=0).astype(BF16)
    spec = lambda s: pl.BlockSpec((None, seq, w), lambda i, h: (i, 0, c0 + s * (heads // nh) + h))
    return pl.pallas_call(
        functools.partial(_sb_attn_kernel, seq=seq, nh=nh),
        grid=(b, heads // nh),
        in_specs=[spec(0), spec(1), spec(2), pl.BlockSpec((2 * blk, blk), lambda i, h: (0, 0))],
        out_specs=pl.BlockSpec((None, seq, w), lambda i, h: (i, 0, h)),
        out_shape=jax.ShapeDtypeStruct((b, seq, heads * HEAD_DIM), BF16),
        scratch_shapes=[pltpu.VMEM((nh, ATT_BLOCK, HEAD_DIM), F32)],
        compiler_params=_params("parallel", "parallel"),
        name="sb_attn",
    )(proj, proj, proj, u2)


def kernel(x, meta_tokens, emb_ln_g, emb_ln_b, w_in, short_conv_w, lambda_q1, lambda_k1, lambda_q2, lambda_k2,
           diff_norm_g, w_out, ln1_g, ln1_b, w_up, ffn_conv_w, w_down, ln2_g, ln2_b):
    bsz, _, d = x.shape
    depth = w_in.shape[0]
    conv_w = short_conv_w.shape[-1]
    d_ff = w_down.shape[1]
    diff_w = (w_in.shape[-1] - 3 * conv_w) // 6
    heads = diff_w // HEAD_DIM
    alpha = (2 * depth) ** 0.25

    meta = jnp.broadcast_to(meta_tokens[None].astype(x.dtype), (bsz, N_META, d))
    tokens = jnp.concatenate([meta, x], axis=1)
    seq = tokens.shape[1]
    rows = bsz * seq
    tm_seq = seq // 3
    tm_big = 2 * tm_seq
    assert seq % 3 == 0 and tm_seq % 16 == 0 and rows % tm_big == 0, "row tiling assumes seq = 3 * 16k rows"
    tr = _pick_tile(rows, (192, 176, 96, 48, 16))
    tn_wide = _pick_tile(math.gcd(6 * diff_w, d), (512, 256, 128))
    tn_conv = _pick_tile(math.gcd(conv_w, d_ff), (256, 128))
    nh = _pick_tile(heads, (ATT_HEADS_PER_STEP, 1))

    w_down_b = w_down.astype(BF16)
    w_out_b = w_out.astype(BF16)

    h, hb = _layer_norm(tokens.reshape(rows, d), emb_ln_g, emb_ln_b, tr=tr)
    for l in range(depth):
        lam_init = _lambda_init(l)
        lam = (jnp.exp(jnp.sum(lambda_q1[l] * lambda_k1[l])) - jnp.exp(jnp.sum(lambda_q2[l] * lambda_k2[l]))
               + lam_init).reshape(1).astype(F32)
        y_conv = _conv_mixer(hb, w_in, short_conv_w, l, width=conv_w, seq=seq, tm=tm_seq, tn=tn_conv)
        proj = _project(hb, w_in, l, col_off=3 * conv_w, n_cols=6 * diff_w, tm=tm_big, tn=tn_wide, out_dtype=BF16)
        proj = proj.reshape(bsz, seq, 6 * diff_w)
        y_diff = _diff_attention(proj, lam, diff_norm_g[l], heads=heads, col_off=0, lam_init=lam_init, nh=nh)
        y_sb = _sb_attention(proj, heads=heads, col_off=3 * diff_w, nh=nh)
        x1 = _out_proj(y_conv, y_diff.reshape(rows, diff_w), y_sb.reshape(rows, diff_w),
                       w_out_b, h, l, alpha=alpha, tm=tm_big, tn=tn_wide)
        h, hb = _layer_norm(x1, ln1_g[l], ln1_b[l], tr=tr)
        act = _ffn_up(hb, w_up, ffn_conv_w, l, d_ff=d_ff, seq=seq, tm=tm_big, tn=tn_conv, sub=2)
        x2 = _ffn_down(act, w_down_b, h, l, alpha=alpha, tm=tm_seq, tn=tn_wide)
        if l + 1 < depth:
            h, hb = _layer_norm(x2, ln2_g[l], ln2_b[l], tr=tr)
    return _layer_norm_drop_meta(x2, ln2_g[depth - 1], ln2_b[depth - 1], bsz=bsz, seq=seq,
                                 tr=_pick_tile(seq - N_META, (256, 128, 64, 16)))
```
